```python
import math
import jax, jax.numpy as jnp
from jax import lax
import numpy as np

D_MODEL = 1024
BATCH = 8
SEQ = 2048
DEPTH = 2
DEC_BATCH = 128
DEC_SEQ = 1
PAST_LEN = 8192
PAGE_SIZE = 128

N_MIXERS = 2
N_CONV_LAYERS = (DEPTH + 1) // 2
N_ATTN_LAYERS = DEPTH // 2
PLE_DIM = 256
EPS = 1e-6
D_CONV = D_MODEL
CONV_WIDTH = 31
N_HEADS = 16
QK_NOPE_DIM = 64
QK_ROPE_DIM = 32
V_HEAD_DIM = 64
Q_LORA = 384
KV_LORA = 256
ROPE_THETA = 10000.0
ATTN_SCALE = 1.0 / math.sqrt(QK_NOPE_DIM + QK_ROPE_DIM)
Q_BLOCK = 128
N_GROUPS = 4
EXPERTS_PER_GROUP = 8
N_EXPERTS = N_GROUPS * EXPERTS_PER_GROUP
TOP_K = 2
D_EXPERT = 256

kernel_name = "hybrid_conv_mla_hmoe_decode_step"


def rmsnorm(x, g):
    xf = x.astype(jnp.float32)
    y = xf * lax.rsqrt(jnp.mean(xf * xf, axis=-1, keepdims=True) + EPS)
    return (y * g.astype(jnp.float32)).astype(x.dtype)


def layernorm(x, g, b):
    xf = x.astype(jnp.float32)
    mu = jnp.mean(xf, axis=-1, keepdims=True)
    var = jnp.mean(jnp.square(xf - mu), axis=-1, keepdims=True)
    y = (xf - mu) * lax.rsqrt(var + EPS) * g.astype(jnp.float32) + b.astype(jnp.float32)
    return y.astype(x.dtype)


def rope(x, pos):
    half = QK_ROPE_DIM // 2
    inv_freq = jnp.power(jnp.float32(ROPE_THETA), -jnp.arange(half, dtype=jnp.float32) / half)
    ang = pos.astype(jnp.float32)[:, None] * inv_freq[None, :]
    ang = ang.reshape(ang.shape[:1] + (1,) * (x.ndim - 3) + (half,))
    cos, sin = jnp.cos(ang), jnp.sin(ang)
    xf = x.astype(jnp.float32)
    x1, x2 = xf[..., :half], xf[..., half:]
    return jnp.concatenate([x1 * cos - x2 * sin, x1 * sin + x2 * cos], axis=-1).astype(x.dtype)


def conformer_conv(h, buf, w_pw1, b_pw1, w_dw, b_dw, ln_g, ln_b, w_pw2):
    u = h @ w_pw1 + b_pw1
    u = u[..., :D_CONV] * jax.nn.sigmoid(u[..., D_CONV:])
    full = jnp.concatenate([buf.astype(u.dtype), u], axis=1)
    z = lax.conv_general_dilated(
        full, w_dw[:, None, :].astype(u.dtype), (1,), 'VALID',
        dimension_numbers=('NWC', 'WIO', 'NWC'), feature_group_count=D_CONV) + b_dw
    z = jax.nn.silu(layernorm(z, ln_g, ln_b))
    return z @ w_pw2, full[:, -(CONV_WIDTH - 1):]


def latent_attend(q_lat, q_pe, c_kv, k_pe, mask):
    s = (jnp.einsum('bqhl,bkl->bhqk', q_lat, c_kv)
         + jnp.einsum('bqhr,bkr->bhqk', q_pe, k_pe)).astype(jnp.float32) * ATTN_SCALE
    s = jnp.where(mask[None, None], s, -jnp.inf)
    p = jax.nn.softmax(s, axis=-1).astype(c_kv.dtype)
    return jnp.einsum('bhqk,bkl->bqhl', p, c_kv)


def mla_mixer(h, pos, past_ckv, past_kpe, w_dq, g_q, w_uq, w_dkv, g_kv, w_uk, w_uv, w_o):
    B, T, _ = h.shape
    c_q = rmsnorm(h @ w_dq, g_q)
    q = (c_q @ w_uq).reshape(B, T, N_HEADS, QK_NOPE_DIM + QK_ROPE_DIM)
    q_pe = rope(q[..., QK_NOPE_DIM:], pos)
    q_lat = jnp.einsum('bthn,lhn->bthl', q[..., :QK_NOPE_DIM], w_uk)
    kv = h @ w_dkv
    c_kv = rmsnorm(kv[..., :KV_LORA], g_kv)
    k_pe = rope(kv[..., KV_LORA:], pos)
    if past_ckv is None:
        nb = T // Q_BLOCK
        qb_lat = q_lat.reshape(B, nb, Q_BLOCK, N_HEADS, KV_LORA).swapaxes(0, 1)
        qb_pe = q_pe.reshape(B, nb, Q_BLOCK, N_HEADS, QK_ROPE_DIM).swapaxes(0, 1)
        k_idx = jnp.arange(T)

        def block(args):
            b, ql, qp = args
            q_idx = b * Q_BLOCK + jnp.arange(Q_BLOCK)
            return latent_attend(ql, qp, c_kv, k_pe, k_idx[None, :] <= q_idx[:, None])

        o_lat = lax.map(block, (jnp.arange(nb), qb_lat, qb_pe))
        o_lat = o_lat.swapaxes(0, 1).reshape(B, T, N_HEADS, KV_LORA)
    else:
        P = past_ckv.shape[1]
        keys_ckv = jnp.concatenate([past_ckv.astype(c_kv.dtype), c_kv], axis=1)
        keys_kpe = jnp.concatenate([past_kpe.astype(k_pe.dtype), k_pe], axis=1)
        mask = jnp.arange(P + T)[None, :] <= P + jnp.arange(T)[:, None]
        o_lat = latent_attend(q_lat, q_pe, keys_ckv, keys_kpe, mask)
    o = jnp.einsum('bthl,lhv->bthv', o_lat, w_uv).reshape(B, T, N_HEADS * V_HEAD_DIM)
    return o @ w_o, c_kv, k_pe


def hier_moe(h, w_group, b_group, w_expert, b_expert, w_gate, w_up, w_down):
    B, T, _ = h.shape
    hf = h.astype(jnp.float32)
    g_prob = jax.nn.softmax(hf @ w_group.astype(jnp.float32) + b_group.astype(jnp.float32), axis=-1)
    grp = jnp.argmax(g_prob, axis=-1)
    g_w = jnp.take_along_axis(g_prob, grp[..., None], axis=-1)
    e_logits = (hf @ w_expert.astype(jnp.float32) + b_expert.astype(jnp.float32)).reshape(
        B, T, N_GROUPS, EXPERTS_PER_GROUP)
    e_in = jnp.take_along_axis(e_logits, grp[..., None, None], axis=2)[:, :, 0]
    top_w, top_i = lax.top_k(jax.nn.softmax(e_in, axis=-1), TOP_K)
    top_w = top_w / jnp.sum(top_w, axis=-1, keepdims=True)
    expert_id = grp[..., None] * EXPERTS_PER_GROUP + top_i
    gates = jnp.sum(jax.nn.one_hot(expert_id, N_EXPERTS, dtype=jnp.float32)
                    * (g_w * top_w)[..., None], axis=-2).astype(h.dtype)
    a = jnp.einsum('btd,edf->btef', h, w_gate)
    u = jnp.einsum('btd,edf->btef', h, w_up)
    return jnp.einsum('btef,efd->btd', jax.nn.silu(a) * u * gates[..., None], w_down)


def setup_inputs(seed: int = 0) -> dict:
    key = jax.random.key(seed)
    ks = jax.random.split(key, 36)
    f32 = jnp.float32

    def nrm(k, shape, scale):
        return scale * jax.random.normal(k, shape, f32)

    def gain(k, shape):
        return 1.0 + 0.05 * jax.random.normal(k, shape, f32)

    n_pages = PAST_LEN // PAGE_SIZE
    n_used = DEC_BATCH * n_pages
    n_pool = n_used + n_used // 4
    page_table = jax.random.permutation(ks[7], n_pool)[:n_used].reshape(DEC_BATCH, n_pages).astype(jnp.int32)
    R = QK_ROPE_DIM
    return {
        "x_prompt": nrm(ks[0], (BATCH, SEQ, D_MODEL), 1.0),
        "x_sample": nrm(ks[1], (DEC_BATCH, DEC_SEQ, D_MODEL), 1.0),
        "p_prompt": nrm(ks[2], (DEPTH, BATCH, SEQ, PLE_DIM), 1.0),
        "p_sample": nrm(ks[3], (DEPTH, DEC_BATCH, DEC_SEQ, PLE_DIM), 1.0),
        "state_conv": nrm(ks[4], (N_CONV_LAYERS, DEC_BATCH, CONV_WIDTH - 1, D_CONV), 0.5),
        "cache_ckv": nrm(ks[5], (n_pool, N_ATTN_LAYERS, PAGE_SIZE, KV_LORA), 1.0),
        "cache_kpe": nrm(ks[6], (n_pool, N_ATTN_LAYERS, PAGE_SIZE, R), 1.0),
        "page_table": page_table,
        "norm_mix": gain(ks[8], (DEPTH, D_MODEL)),
        "norm_ffn": gain(ks[9], (DEPTH, D_MODEL)),
        "norm_final": gain(ks[10], (D_MODEL,)),
        "conv_w_pw1": nrm(ks[11], (N_CONV_LAYERS, D_MODEL, 2 * D_CONV), D_MODEL ** -0.5),
        "conv_b_pw1": nrm(ks[12], (N_CONV_LAYERS, 2 * D_CONV), 0.02),
        "conv_w_dw": nrm(ks[13], (N_CONV_LAYERS, CONV_WIDTH, D_CONV), CONV_WIDTH ** -0.5),
        "conv_b_dw": nrm(ks[14], (N_CONV_LAYERS, D_CONV), 0.02),
        "conv_ln_g": gain(ks[15], (N_CONV_LAYERS, D_CONV)),
        "conv_ln_b": nrm(ks[16], (N_CONV_LAYERS, D_CONV), 0.02),
        "conv_w_pw2": nrm(ks[17], (N_CONV_LAYERS, D_CONV, D_MODEL), D_CONV ** -0.5),
        "mla_w_dq": nrm(ks[18], (N_ATTN_LAYERS, D_MODEL, Q_LORA), D_MODEL ** -0.5),
        "mla_g_q": gain(ks[19], (N_ATTN_LAYERS, Q_LORA)),
        "mla_w_uq": nrm(ks[20], (N_ATTN_LAYERS, Q_LORA, N_HEADS * (QK_NOPE_DIM + R)), Q_LORA ** -0.5),
        "mla_w_dkv": nrm(ks[21], (N_ATTN_LAYERS, D_MODEL, KV_LORA + R), D_MODEL ** -0.5),
        "mla_g_kv": gain(ks[22], (N_ATTN_LAYERS, KV_LORA)),
        "mla_w_uk": nrm(ks[23], (N_ATTN_LAYERS, KV_LORA, N_HEADS, QK_NOPE_DIM), KV_LORA ** -0.5),
        "mla_w_uv": nrm(ks[24], (N_ATTN_LAYERS, KV_LORA, N_HEADS, V_HEAD_DIM), KV_LORA ** -0.5),
        "mla_w_o": nrm(ks[25], (N_ATTN_LAYERS, N_HEADS * V_HEAD_DIM, D_MODEL), (N_HEADS * V_HEAD_DIM) ** -0.5),
        "moe_w_group": nrm(ks[26], (DEPTH, D_MODEL, N_GROUPS), D_MODEL ** -0.5),
        "moe_b_group": nrm(ks[27], (DEPTH, N_GROUPS), 0.01),
        "moe_w_expert": nrm(ks[28], (DEPTH, D_MODEL, N_EXPERTS), D_MODEL ** -0.5),
        "moe_b_expert": nrm(ks[29], (DEPTH, N_EXPERTS), 0.01),
        "moe_w_gate": nrm(ks[30], (DEPTH, N_EXPERTS, D_MODEL, D_EXPERT), D_MODEL ** -0.5),
        "moe_w_up": nrm(ks[31], (DEPTH, N_EXPERTS, D_MODEL, D_EXPERT), D_MODEL ** -0.5),
        "moe_w_down": nrm(ks[32], (DEPTH, N_EXPERTS, D_EXPERT, D_MODEL), D_EXPERT ** -0.5),
        "ple_w_in": nrm(ks[33], (DEPTH, PLE_DIM, D_MODEL), PLE_DIM ** -0.5),
        "ple_g": gain(ks[34], (DEPTH, D_MODEL)),
        "ple_w_gate": nrm(ks[35], (DEPTH, D_MODEL, D_MODEL), D_MODEL ** -0.5),
    }


def reference(x_prompt, x_sample, p_prompt, p_sample, state_conv, cache_ckv, cache_kpe, page_table,
              norm_mix, norm_ffn, norm_final,
              conv_w_pw1, conv_b_pw1, conv_w_dw, conv_b_dw, conv_ln_g, conv_ln_b, conv_w_pw2,
              mla_w_dq, mla_g_q, mla_w_uq, mla_w_dkv, mla_g_kv, mla_w_uk, mla_w_uv, mla_w_o,
              moe_w_group, moe_b_group, moe_w_expert, moe_b_expert, moe_w_gate, moe_w_up, moe_w_down,
              ple_w_in, ple_g, ple_w_gate):

    def run(x, p, pos, conv_bufs, past_fn):
        h = x
        new_conv, new_ckv, new_kpe = [], [], []
        for i in range(DEPTH):
            j = i // N_MIXERS
            hn = rmsnorm(h, norm_mix[i])
            if i % N_MIXERS == 0:
                mix, buf = conformer_conv(hn, conv_bufs[j], conv_w_pw1[j], conv_b_pw1[j], conv_w_dw[j],
                                          conv_b_dw[j], conv_ln_g[j], conv_ln_b[j], conv_w_pw2[j])
                new_conv.append(buf)
            else:
                past_ckv, past_kpe = past_fn(j)
                mix, c_kv, k_pe = mla_mixer(hn, pos, past_ckv, past_kpe, mla_w_dq[j], mla_g_q[j], mla_w_uq[j],
                                            mla_w_dkv[j], mla_g_kv[j], mla_w_uk[j], mla_w_uv[j], mla_w_o[j])
                new_ckv.append(c_kv)
                new_kpe.append(k_pe)
            h = h + mix
            h = h + hier_moe(rmsnorm(h, norm_ffn[i]), moe_w_group[i], moe_b_group[i], moe_w_expert[i],
                             moe_b_expert[i], moe_w_gate[i], moe_w_up[i], moe_w_down[i])
            h = h + (p[i] @ ple_w_in[i]) * jax.nn.sigmoid(rmsnorm(h, ple_g[i]) @ ple_w_gate[i])
        return rmsnorm(h, norm_final), jnp.stack(new_conv), jnp.stack(new_ckv), jnp.stack(new_kpe)

    pos_prompt = jnp.arange(x_prompt.shape[1], dtype=jnp.int32)
    zero_bufs = jnp.zeros((N_CONV_LAYERS, x_prompt.shape[0], CONV_WIDTH - 1, D_CONV), x_prompt.dtype)
    y_prompt, conv_state_prompt, ckv_prompt, kpe_prompt = run(
        x_prompt, p_prompt, pos_prompt, zero_bufs, lambda j: (None, None))

    n_seq = page_table.shape[0]
    pos_sample = PAST_LEN + jnp.arange(x_sample.shape[1], dtype=jnp.int32)

    def gather_past(j):
        ckv = cache_ckv[page_table, j].reshape(n_seq, -1, KV_LORA)
        kpe = cache_kpe[page_table, j].reshape(n_seq, -1, QK_ROPE_DIM)
        return ckv, kpe

    y_sample, conv_state_sample, ckv_sample, kpe_sample = run(
        x_sample, p_sample, pos_sample, state_conv, gather_past)

    return (y_prompt, y_sample, conv_state_prompt, conv_state_sample,
            ckv_prompt, kpe_prompt, ckv_sample, kpe_sample)
```

```python
import functools
import math

import jax
import jax.numpy as jnp
import numpy as np
from jax import lax
from jax.experimental import pallas as pl
from jax.experimental.pallas import tpu as pltpu

F32 = jnp.float32
BF16 = jnp.bfloat16

EPS = 1e-6
ROPE_THETA = 10000.0
CONV_HALO = 32
LANES = 128
VMEM_LIMIT = 56 * 1024 * 1024


def _cp(*sem):
    return pltpu.CompilerParams(dimension_semantics=sem, vmem_limit_bytes=VMEM_LIMIT)


def _rms(x, g):
    return x * lax.rsqrt(jnp.mean(x * x, axis=-1, keepdims=True) + EPS) * g


def _dot(a, b):
    return jnp.dot(a, b, preferred_element_type=F32)


def _dot_nt(a, b):
    return lax.dot_general(a, b, (((1,), (1,)), ((), ())), preferred_element_type=F32)


def _row_tile(m, target):
    t = min(m, target)
    while m % t:
        t //= 2
    return t


def _pw1_glu_kernel(x_ref, g_ref, w_ref, b_ref, o_ref):
    d = o_ref.shape[-1]
    hn = _rms(x_ref[...], g_ref[...]).astype(BF16)
    a = _dot(hn, w_ref[:, :d]) + b_ref[:, :d]
    b = _dot(hn, w_ref[:, d:]) + b_ref[:, d:]
    o_ref[...] = a * jax.nn.sigmoid(b)


def _pw1_glu(x, g, w_bf, b):
    m, d = x.shape
    tm = _row_tile(m, 512)
    return pl.pallas_call(
        _pw1_glu_kernel,
        grid=(m // tm,),
        in_specs=[
            pl.BlockSpec((tm, d), lambda i: (i, 0)),
            pl.BlockSpec((1, d), lambda i: (0, 0)),
            pl.BlockSpec((d, 2 * d), lambda i: (0, 0)),
            pl.BlockSpec((1, 2 * d), lambda i: (0, 0)),
        ],
        out_specs=pl.BlockSpec((tm, d), lambda i: (i, 0)),
        out_shape=jax.ShapeDtypeStruct((m, d), F32),
        compiler_params=_cp("parallel"),
        name="pw1_glu",
    )(x, g.reshape(1, d), w_bf, b.reshape(1, 2 * d))


def _ln_silu(z, g, b):
    mu = jnp.mean(z, axis=-1, keepdims=True)
    zc = z - mu
    var = jnp.mean(zc * zc, axis=-1, keepdims=True)
    y = zc * lax.rsqrt(var + EPS) * g + b
    return y * jax.nn.sigmoid(y)


def _conv_prompt_kernel(cur_ref, halo_ref, h_ref, wdw_ref, bdw_ref, lng_ref, lnb_ref, w2_ref,
                        o_ref, full_ref, z_ref, *, width, rows, lanes):
    tq, d = cur_ref.shape[1], cur_ref.shape[2]
    i = pl.program_id(1)
    full_ref[0:CONV_HALO, :] = jnp.where(i > 0, halo_ref[0], 0.0)
    full_ref[CONV_HALO:CONV_HALO + tq, :] = cur_ref[0]
    off = CONV_HALO - (width - 1)
    for r0 in range(0, tq, rows):
        for l0 in range(0, d, lanes):
            acc = jnp.zeros((rows, lanes), F32)
            for k in range(width):
                acc = acc + (full_ref[r0 + off + k:r0 + off + k + rows, l0:l0 + lanes]
                             * wdw_ref[k:k + 1, l0:l0 + lanes])
            z_ref[r0:r0 + rows, l0:l0 + lanes] = acc
    y = _ln_silu(z_ref[...] + bdw_ref[...], lng_ref[...], lnb_ref[...])
    o_ref[0] = h_ref[0] + _dot(y.astype(BF16), w2_ref[...])


def _conv_prompt(glu, h, w_dw, b_dw, ln_g, ln_b, w2_bf):
    bsz, t, d = glu.shape
    width = w_dw.shape[0]
    tq = _row_tile(t, 512)
    hb = tq // CONV_HALO
    wpad = jnp.zeros((CONV_HALO, d), F32).at[:width].set(w_dw)
    kern = functools.partial(_conv_prompt_kernel, width=width, rows=min(64, tq), lanes=256)
    vec = lambda a: a.reshape(1, d)
    cvec = pl.BlockSpec((1, d), lambda b, i: (0, 0))
    return pl.pallas_call(
        kern,
        grid=(bsz, t // tq),
        in_specs=[
            pl.BlockSpec((1, tq, d), lambda b, i: (b, i, 0)),
            pl.BlockSpec((1, CONV_HALO, d), lambda b, i: (b, jnp.maximum(i * hb - 1, 0), 0)),
            pl.BlockSpec((1, tq, d), lambda b, i: (b, i, 0)),
            pl.BlockSpec((CONV_HALO, d), lambda b, i: (0, 0)),
            cvec, cvec, cvec,
            pl.BlockSpec((d, d), lambda b, i: (0, 0)),
        ],
        out_specs=pl.BlockSpec((1, tq, d), lambda b, i: (b, i, 0)),
        out_shape=jax.ShapeDtypeStruct((bsz, t, d), F32),
        scratch_shapes=[pltpu.VMEM((tq + CONV_HALO, d), F32), pltpu.VMEM((tq, d), F32)],
        compiler_params=_cp("parallel", "arbitrary"),
        name="conv_prompt",
    )(glu, glu, h, wpad, vec(b_dw), vec(ln_g), vec(ln_b), w2_bf)


def _conv_sample_kernel(buf_ref, u_ref, h_ref, wdw_ref, bdw_ref, lng_ref, lnb_ref, w2_ref, o_ref,
                        *, width):
    acc = u_ref[...] * wdw_ref[width - 1:width, :]
    for k in range(width - 1):
        acc = acc + buf_ref[k] * wdw_ref[k:k + 1, :]
    y = _ln_silu(acc + bdw_ref[...], lng_ref[...], lnb_ref[...])
    o_ref[...] = h_ref[...] + _dot(y.astype(BF16), w2_ref[...])


def _conv_sample(buf_t, u, h, w_dw, b_dw, ln_g, ln_b, w2_bf):
    nb, bsz, d = buf_t.shape
    width = w_dw.shape[0]
    bb = _row_tile(bsz, 32)
    wpad = jnp.zeros((CONV_HALO, d), F32).at[:width].set(w_dw)
    vec = lambda a: a.reshape(1, d)
    cvec = pl.BlockSpec((1, d), lambda i: (0, 0))
    row = pl.BlockSpec((bb, d), lambda i: (i, 0))
    return pl.pallas_call(
        functools.partial(_conv_sample_kernel, width=width),
        grid=(bsz // bb,),
        in_specs=[
            pl.BlockSpec((nb, bb, d), lambda i: (0, i, 0)),
            row, row,
            pl.BlockSpec((CONV_HALO, d), lambda i: (0, 0)),
            cvec, cvec, cvec,
            pl.BlockSpec((d, d), lambda i: (0, 0)),
        ],
        out_specs=row,
        out_shape=jax.ShapeDtypeStruct((bsz, d), F32),
        compiler_params=_cp("parallel"),
        name="conv_sample",
    )(buf_t, u, h, wpad, vec(b_dw), vec(ln_g), vec(ln_b), w2_bf)


def _router_kernel(x_ref, g_ref, w_ref, b_ref, o_ref, *, n_groups, per_group):
    hn = _rms(x_ref[...], g_ref[...])
    logits = jnp.dot(hn, w_ref[...], preferred_element_type=F32,
                     precision=lax.Precision.HIGHEST) + b_ref[...]
    lane = lax.broadcasted_iota(jnp.int32, logits.shape, 1)
    neg = jnp.float32(-jnp.inf)
    big = jnp.int32(1 << 20)

    def first_argmax(v):
        m = jnp.max(v, axis=-1, keepdims=True)
        return m, jnp.min(jnp.where(v == m, lane, big), axis=-1, keepdims=True)

    gl = jnp.where(lane < n_groups, logits, neg)
    gmax, grp = first_argmax(gl)
    g_w = 1.0 / jnp.sum(jnp.exp(gl - gmax), axis=-1, keepdims=True)
    lo = n_groups + grp * per_group
    el = jnp.where((lane >= lo) & (lane < lo + per_group), logits, neg)
    m1, i1 = first_argmax(el)
    el2 = jnp.where(lane == i1, neg, el)
    m2, i2 = first_argmax(el2)
    e2 = jnp.exp(m2 - m1)
    w1 = g_w / (1.0 + e2)
    w2 = g_w * e2 / (1.0 + e2)
    id1 = (i1 - n_groups).astype(F32)
    id2 = (i2 - n_groups).astype(F32)
    o_ref[...] = jnp.where(lane == 0, id1, jnp.where(lane == 1, id2,
                           jnp.where(lane == 2, w1, jnp.where(lane == 3, w2, 0.0))))


def _router(h, g, w_group, b_group, w_expert, b_expert):
    m, d = h.shape
    n_groups, n_experts = w_group.shape[1], w_expert.shape[1]
    tm = _row_tile(m, 512)
    w = jnp.zeros((d, LANES), F32).at[:, :n_groups].set(w_group)
    w = w.at[:, n_groups:n_groups + n_experts].set(w_expert)
    b = jnp.zeros((1, LANES), F32).at[0, :n_groups].set(b_group)
    b = b.at[0, n_groups:n_groups + n_experts].set(b_expert)
    kern = functools.partial(_router_kernel, n_groups=n_groups, per_group=n_experts // n_groups)
    return pl.pallas_call(
        kern,
        grid=(m // tm,),
        in_specs=[
            pl.BlockSpec((tm, d), lambda i: (i, 0)),
            pl.BlockSpec((1, d), lambda i: (0, 0)),
            pl.BlockSpec((d, LANES), lambda i: (0, 0)),
            pl.BlockSpec((1, LANES), lambda i: (0, 0)),
        ],
        out_specs=pl.BlockSpec((tm, LANES), lambda i: (i, 0)),
        out_shape=jax.ShapeDtypeStruct((m, LANES), F32),
        compiler_params=_cp("parallel"),
        name="moe_router",
    )(h, g.reshape(1, d), w, b)


def _route_plan(route, n_experts, tm):
    m = route.shape[0]
    eid = route[:, :2].astype(jnp.int32).reshape(-1)
    onehot = (eid[:, None] == jnp.arange(n_experts)[None, :]).astype(jnp.int32)
    csum = jnp.cumsum(onehot, axis=0)
    counts = csum[-1]
    rank = jnp.take_along_axis(csum, eid[:, None], axis=1)[:, 0] - 1
    tiles_per = (counts + tm - 1) // tm
    tile_end = jnp.cumsum(tiles_per)
    start = (tile_end - tiles_per) * tm
    pos = start[eid] + rank
    n_tiles = (2 * m) // tm + n_experts
    row_tok = jnp.zeros((n_tiles * tm,), jnp.int32).at[pos].set(jnp.arange(2 * m, dtype=jnp.int32) // 2)
    tile_ids = jnp.arange(n_tiles, dtype=jnp.int32)
    tile_expert = jnp.minimum(jnp.searchsorted(tile_end, tile_ids, side="right"),
                              n_experts - 1).astype(jnp.int32)
    n_used = tile_end[-1].astype(jnp.int32).reshape(1)
    return pos.astype(jnp.int32), row_tok, tile_expert, n_used


def _moe_expert_kernel(te_ref, nu_ref, tok_ref, x_hbm, g_ref, wg_ref, wu_ref, wd_ref, o_ref,
                       xbuf, sem, *, tm):
    i = pl.program_id(0)
    n_used = nu_ref[0]

    def row_copy(tile, slot, r):
        return pltpu.make_async_copy(x_hbm.at[pl.ds(tok_ref[tile * tm + r], 1), :],
                                     xbuf.at[slot, pl.ds(r, 1), :], sem.at[slot])

    def start_tile(tile, slot):
        def body(r, c):
            row_copy(tile, slot, r).start()
            return c
        lax.fori_loop(0, tm, body, 0)

    def wait_tile(tile, slot):
        def body(r, c):
            row_copy(tile, slot, r).wait()
            return c
        lax.fori_loop(0, tm, body, 0)

    @pl.when((i == 0) & (n_used > 0))
    def _():
        start_tile(0, 0)

    @pl.when(i + 1 < n_used)
    def _():
        start_tile(i + 1, (i + 1) % 2)

    @pl.when(i < n_used)
    def _():
        slot = i % 2
        wait_tile(i, slot)
        x = _rms(xbuf[slot], g_ref[...]).astype(BF16)
        a = _dot(x, wg_ref[0])
        u = _dot(x, wu_ref[0])
        act = (a * jax.nn.sigmoid(a) * u).astype(BF16)
        o_ref[...] = _dot(act, wd_ref[0])

    @pl.when(i >= n_used)
    def _():
        o_ref[...] = jnp.zeros_like(o_ref)


def _moe_experts(h, g, plan, wg_bf, wu_bf, wd_bf, tm):
    m, d = h.shape
    _, row_tok, tile_expert, n_used = plan
    n_tiles = row_tok.shape[0] // tm
    f = wg_bf.shape[2]
    grid_spec = pltpu.PrefetchScalarGridSpec(
        num_scalar_prefetch=3,
        grid=(n_tiles,),
        in_specs=[
            pl.BlockSpec(memory_space=pl.ANY),
            pl.BlockSpec((1, d), lambda i, te, nu, tok: (0, 0)),
            pl.BlockSpec((1, d, f), lambda i, te, nu, tok: (te[i], 0, 0)),
            pl.BlockSpec((1, d, f), lambda i, te, nu, tok: (te[i], 0, 0)),
            pl.BlockSpec((1, f, d), lambda i, te, nu, tok: (te[i], 0, 0)),
        ],
        out_specs=pl.BlockSpec((tm, d), lambda i, te, nu, tok: (i, 0)),
        scratch_shapes=[pltpu.VMEM((2, tm, d), F32), pltpu.SemaphoreType.DMA((2,))],
    )
    return pl.pallas_call(
        functools.partial(_moe_expert_kernel, tm=tm),
        grid_spec=grid_spec,
        out_shape=jax.ShapeDtypeStruct((n_tiles * tm, d), F32),
        compiler_params=_cp("arbitrary"),
        name="moe_experts",
    )(tile_expert, n_used, row_tok, h, g.reshape(1, d), wg_bf, wu_bf, wd_bf)


def _combine_ple_kernel(pos_ref, h_ref, route_ref, y_hbm, p_ref, win_ref, g_ref, wgate_ref,
                        gfin_ref, o_ref, ybuf, sem, *, tm, final_norm):
    i = pl.program_id(0)
    n = pl.num_programs(0)
    m = n * tm

    def row_copy(tile, slot, k, r):
        return pltpu.make_async_copy(y_hbm.at[pl.ds(pos_ref[k * m + tile * tm + r], 1), :],
                                     ybuf.at[slot, pl.ds(k * tm + r, 1), :], sem.at[slot])

    def start_tile(tile, slot):
        def body(r, c):
            row_copy(tile, slot, 0, r).start()
            row_copy(tile, slot, 1, r).start()
            return c
        lax.fori_loop(0, tm, body, 0)

    def wait_tile(tile, slot):
        def body(r, c):
            row_copy(tile, slot, 0, r).wait()
            row_copy(tile, slot, 1, r).wait()
            return c
        lax.fori_loop(0, tm, body, 0)

    @pl.when(i == 0)
    def _():
        start_tile(0, 0)

    @pl.when(i + 1 < n)
    def _():
        start_tile(i + 1, (i + 1) % 2)

    slot = i % 2
    wait_tile(i, slot)
    d = h_ref.shape[-1]
    route = route_ref[...]
    h2 = h_ref[...] + route[:, 2:3] * ybuf[slot, 0:tm, :] + route[:, 3:4] * ybuf[slot, tm:2 * tm, :]
    gate = jax.nn.sigmoid(_dot(_rms(h2, g_ref[...]).astype(BF16), wgate_ref[...]))
    h3 = h2 + _dot(p_ref[...].astype(BF16), win_ref[...]) * gate
    if final_norm:
        h3 = _rms(h3, gfin_ref[...])
    o_ref[...] = h3


def _combine_ple(h, route, pos, y_sorted, p, win_bf, g_ple, wgate_bf, g_final, tm, final_norm):
    m, d = h.shape
    pd = p.shape[1]
    grid_spec = pltpu.PrefetchScalarGridSpec(
        num_scalar_prefetch=1,
        grid=(m // tm,),
        in_specs=[
            pl.BlockSpec((tm, d), lambda i, pos: (i, 0)),
            pl.BlockSpec((tm, LANES), lambda i, pos: (i, 0)),
            pl.BlockSpec(memory_space=pl.ANY),
            pl.BlockSpec((tm, pd), lambda i, pos: (i, 0)),
            pl.BlockSpec((pd, d), lambda i, pos: (0, 0)),
            pl.BlockSpec((1, d), lambda i, pos: (0, 0)),
            pl.BlockSpec((d, d), lambda i, pos: (0, 0)),
            pl.BlockSpec((1, d), lambda i, pos: (0, 0)),
        ],
        out_specs=pl.BlockSpec((tm, d), lambda i, pos: (i, 0)),
        scratch_shapes=[pltpu.VMEM((2, 2 * tm, d), F32), pltpu.SemaphoreType.DMA((2,))],
    )
    return pl.pallas_call(
        functools.partial(_combine_ple_kernel, tm=tm, final_norm=final_norm),
        grid_spec=grid_spec,
        out_shape=jax.ShapeDtypeStruct((m, d), F32),
        compiler_params=_cp("arbitrary"),
        name="moe_combine_ple",
    )(pos, h, route, y_sorted, p, win_bf, g_ple.reshape(1, d), wgate_bf, g_final.reshape(1, d))


def _moe_ple_layer(h, p, norm_ffn, w_group, b_group, w_expert, b_expert, wg_bf, wu_bf, wd_bf,
                   win_bf, g_ple, wgate_bf, g_final, final_norm):
    m = h.shape[0]
    n_experts = w_expert.shape[1]
    tm_e = 256 if m >= 4096 else 32
    route = _router(h, norm_ffn, w_group, b_group, w_expert, b_expert)
    plan = _route_plan(route, n_experts, tm_e)
    y_sorted = _moe_experts(h, norm_ffn, plan, wg_bf, wu_bf, wd_bf, tm_e)
    tm_c = _row_tile(m, 256)
    pos_km = plan[0].reshape(m, 2).T.reshape(-1)
    return _combine_ple(h, route, pos_km, y_sorted, p, win_bf, g_ple, wgate_bf, g_final, tm_c,
                        final_norm)


def _rope_block(x, c, s):
    return x * c + pltpu.roll(x, 64, 1) * s


def _mla_proj_kernel(x_ref, gmix_ref, wdq_ref, gq_ref, wuq_ref, wdkv_ref, gkv_ref, cos_ref, sin_ref,
                     wuk_ref, wuv_ref, qn_ref, qr_ref, ckv_ref, kr_ref, *kv_refs,
                     scale, kv_lora, n_pairs, with_kv):
    hn = _rms(x_ref[...], gmix_ref[...]).astype(BF16)
    cq = _rms(_dot(hn, wdq_ref[...]), gq_ref[...]).astype(BF16)
    q = _dot(cq, wuq_ref[...]) * scale
    nn = n_pairs * LANES
    c, s = cos_ref[...], sin_ref[...]
    qn_ref[...] = q[:, :nn].astype(qn_ref.dtype)
    for p in range(n_pairs):
        blk = q[:, nn + p * LANES:nn + (p + 1) * LANES]
        qr_ref[:, p * LANES:(p + 1) * LANES] = _rope_block(blk, c, s).astype(qr_ref.dtype)
    kv = _dot(hn, wdkv_ref[...])
    ckv = _rms(kv[:, :kv_lora], gkv_ref[...])
    kr = _rope_block(kv[:, kv_lora:], c, s)
    ckv_ref[...] = ckv
    kr_ref[...] = kr
    if with_kv:
        kn_ref, v_ref, krb_ref = kv_refs
        cb = ckv.astype(BF16)
        kn_ref[...] = _dot(cb, wuk_ref[...]).astype(BF16)
        v_ref[...] = _dot(cb, wuv_ref[...]).astype(BF16)
        krb_ref[...] = kr.astype(BF16)


def _rope_tables(pos, rope_dim):
    half = rope_dim // 2
    inv_freq = jnp.power(jnp.float32(ROPE_THETA), -jnp.arange(half, dtype=F32) / half)
    ang = pos.astype(F32)[:, None] * inv_freq[None, :]
    cos, sin = jnp.cos(ang), jnp.sin(ang)
    z = jnp.zeros((pos.shape[0], 2 * half), F32)
    c = jnp.concatenate([cos, cos, z, cos, cos, z], axis=1)
    s = jnp.concatenate([-sin, -sin, z, sin, sin, z], axis=1)
    return c, s


def _mla_weights(w_dq, w_uq, w_dkv, w_uk, w_uv, n_heads, nope, rope_dim, kv_lora):
    half = rope_dim // 2
    assert 2 * nope == LANES and 4 * half <= LANES // 2
    n_pairs = n_heads // 2
    hd = nope + rope_dim
    idx_n = np.array([h * hd + n for h in range(n_heads) for n in range(nope)])
    w_qn = w_uq[:, idx_n]
    zq = jnp.zeros((w_uq.shape[0], LANES // 2 - 2 * half), F32)
    blocks = []
    for p in range(n_pairs):
        h0, h1 = 2 * p, 2 * p + 1
        x1 = lambda h: w_uq[:, h * hd + nope:h * hd + nope + half]
        x2 = lambda h: w_uq[:, h * hd + nope + half:h * hd + hd]
        blocks += [x1(h0), x1(h1), zq, x2(h0), x2(h1), zq]
    w_q = jnp.concatenate([w_qn] + blocks, axis=1).astype(BF16)
    zk = jnp.zeros((w_dkv.shape[0], LANES // 2 - 2 * half), F32)
    k1 = w_dkv[:, kv_lora:kv_lora + half]
    k2 = w_dkv[:, kv_lora + half:]
    w_kv = jnp.concatenate([w_dkv[:, :kv_lora], k1, k1, zk, k2, k2, zk], axis=1).astype(BF16)
    w_k = w_uk.reshape(kv_lora, n_heads * nope).astype(BF16)
    w_v = w_uv.reshape(kv_lora, -1).astype(BF16)
    return w_dq.astype(BF16), w_q, w_kv, w_k, w_v


def _mla_proj(h, g_mix, wts, g_q, g_kv, cos, sin, per_row_tables, scale, with_kv, q_dtype):
    m, d = h.shape
    w_dq, w_q, w_kv, w_k, w_v = wts
    q_lora = w_dq.shape[1]
    kv_lora = w_k.shape[0]
    n_pairs = w_k.shape[1] // LANES
    nn = n_pairs * LANES
    tm = _row_tile(m, 512)
    if per_row_tables:
        t = cos.shape[0]
        nt = t // tm
        tab = pl.BlockSpec((tm, LANES), lambda i: (i % nt, 0))
    else:
        tab = pl.BlockSpec((1, LANES), lambda i: (0, 0))
    const = lambda a: pl.BlockSpec(a.shape, lambda i: (0,) * a.ndim)
    row = lambda n: pl.BlockSpec((tm, n), lambda i: (i, 0))
    out_shape = [jax.ShapeDtypeStruct((m, nn), q_dtype), jax.ShapeDtypeStruct((m, nn), q_dtype),
                 jax.ShapeDtypeStruct((m, kv_lora), F32), jax.ShapeDtypeStruct((m, LANES), F32)]
    out_specs = [row(nn), row(nn), row(kv_lora), row(LANES)]
    if with_kv:
        out_shape += [jax.ShapeDtypeStruct((m, nn), BF16), jax.ShapeDtypeStruct((m, w_v.shape[1]), BF16),
                      jax.ShapeDtypeStruct((m, LANES), BF16)]
        out_specs += [row(nn), row(w_v.shape[1]), row(LANES)]
    gq2, gkv2, gm2 = g_q.reshape(1, q_lora), g_kv.reshape(1, kv_lora), g_mix.reshape(1, d)
    kern = functools.partial(_mla_proj_kernel, scale=scale, kv_lora=kv_lora, n_pairs=n_pairs,
                             with_kv=with_kv)
    return pl.pallas_call(
        kern,
        grid=(m // tm,),
        in_specs=[row(d), const(gm2), const(w_dq), const(gq2), const(w_q), const(w_kv), const(gkv2),
                  tab, tab, const(w_k), const(w_v)],
        out_specs=out_specs,
        out_shape=out_shape,
        compiler_params=_cp("parallel"),
        name="mla_proj",
    )(h, gm2, w_dq, gq2, w_q, w_kv, gkv2, cos, sin, w_k, w_v)


def _attn_prompt_kernel(qn_ref, qr_ref, kn_ref, kr_ref, v_ref, o_ref, m_ref, l_ref, acc_ref, *, tq):
    i = pl.program_id(2)
    lane = lax.broadcasted_iota(jnp.int32, (1, LANES), 1)
    half = LANES // 2
    qn, qr = qn_ref[0], qr_ref[0]
    qs = []
    for hh in range(2):
        mask_n = (lane // half) == hh
        mask_r = ((lane % half) // 16) == hh
        qs.append(jnp.concatenate([jnp.where(mask_n, qn, 0), jnp.where(mask_r, qr, 0)], axis=1))
    m_ref[...] = jnp.full(m_ref.shape, -jnp.inf, F32)
    l_ref[...] = jnp.zeros(l_ref.shape, F32)
    acc_ref[...] = jnp.zeros(acc_ref.shape, F32)

    def block(j, masked):
        r0 = pl.multiple_of(j * tq, tq)
        k = jnp.concatenate([kn_ref[0, pl.ds(r0, tq), :], kr_ref[0, pl.ds(r0, tq), :]], axis=1)
        v = v_ref[0, pl.ds(r0, tq), :]
        for hh in range(2):
            s = _dot_nt(qs[hh], k)
            if masked:
                row = lax.broadcasted_iota(jnp.int32, s.shape, 0)
                col = lax.broadcasted_iota(jnp.int32, s.shape, 1)
                s = jnp.where(col <= row, s, -jnp.inf)
            m_old = m_ref[hh]
            m_new = jnp.maximum(m_old, jnp.max(s, axis=-1, keepdims=True))
            alpha = jnp.exp(m_old - m_new)
            p = jnp.exp(s - m_new)
            l_ref[hh] = alpha * l_ref[hh] + jnp.sum(p, axis=-1, keepdims=True)
            acc_ref[hh] = alpha * acc_ref[hh] + _dot(p.astype(BF16), v)
            m_ref[hh] = m_new

    def body(j, c):
        block(j, False)
        return c

    lax.fori_loop(0, i, body, 0)
    block(i, True)
    o0 = acc_ref[0] / l_ref[0]
    o1 = acc_ref[1] / l_ref[1]
    o_ref[0] = jnp.where((lane // half) == 0, o0, o1).astype(o_ref.dtype)


def _attn_prompt(qn, qr, kn, kr, v, bsz, t):
    n_pairs = qn.shape[-1] // LANES
    tq = _row_tile(t, 512)
    r3 = lambda a: a.reshape(bsz, t, a.shape[-1])
    qspec = pl.BlockSpec((1, tq, LANES), lambda b, p, i: (b, i, p))
    kspec = pl.BlockSpec((1, t, LANES), lambda b, p, i: (b, 0, p))
    return pl.pallas_call(
        functools.partial(_attn_prompt_kernel, tq=tq),
        grid=(bsz, n_pairs, t // tq),
        in_specs=[qspec, qspec, kspec, pl.BlockSpec((1, t, LANES), lambda b, p, i: (b, 0, 0)), kspec],
        out_specs=qspec,
        out_shape=jax.ShapeDtypeStruct((bsz, t, n_pairs * LANES), BF16),
        scratch_shapes=[pltpu.VMEM((2, tq, 1), F32), pltpu.VMEM((2, tq, 1), F32),
                        pltpu.VMEM((2, tq, LANES), F32)],
        compiler_params=_cp("parallel", "parallel", "arbitrary"),
        name="attn_prompt",
    )(r3(qn), r3(qr), r3(kn), r3(kr), r3(v))


def _out_proj_kernel(h_ref, o_ref, w_ref, y_ref):
    y_ref[...] = h_ref[...] + _dot(o_ref[...], w_ref[...])


def _out_proj(h, o_bf, wo_bf):
    m, d = h.shape
    n = o_bf.shape[1]
    tm = _row_tile(m, 512)
    return pl.pallas_call(
        _out_proj_kernel,
        grid=(m // tm,),
        in_specs=[pl.BlockSpec((tm, d), lambda i: (i, 0)), pl.BlockSpec((tm, n), lambda i: (i, 0)),
                  pl.BlockSpec((n, d), lambda i: (0, 0))],
        out_specs=pl.BlockSpec((tm, d), lambda i: (i, 0)),
        out_shape=jax.ShapeDtypeStruct((m, d), F32),
        compiler_params=_cp("parallel"),
        name="attn_out_proj",
    )(h, o_bf, wo_bf)


def _q_absorb_kernel(qn_ref, wukt_ref, o_ref, *, n_pairs, kv_lora):
    lane = lax.broadcasted_iota(jnp.int32, (1, LANES), 1)
    half = LANES // 2
    for p in range(n_pairs):
        blk = qn_ref[:, p * LANES:(p + 1) * LANES]
        for hh in range(2):
            qm = jnp.where((lane // half) == hh, blk, 0.0).astype(BF16)
            h = 2 * p + hh
            o_ref[:, h * kv_lora:(h + 1) * kv_lora] = _dot(qm, wukt_ref[p])


def _q_absorb(qn, wukt_bf):
    m = qn.shape[0]
    n_pairs, _, kv_lora = wukt_bf.shape
    return pl.pallas_call(
        functools.partial(_q_absorb_kernel, n_pairs=n_pairs, kv_lora=kv_lora),
        out_shape=jax.ShapeDtypeStruct((m, 2 * n_pairs * kv_lora), F32),
        compiler_params=pltpu.CompilerParams(vmem_limit_bytes=VMEM_LIMIT),
        name="q_absorb",
    )(qn, wukt_bf)


def _attn_decode_kernel(pt_ref, qlat_ref, qpe_ref, cnew_ref, knew_ref, ckv_hbm, kpe_hbm, o_ref,
                        cbuf, kbuf, sem, *, n_pages, page, layer):
    b = pl.program_id(0)
    nb = pl.num_programs(0)

    def copies(seq, slot, j):
        pg = pt_ref[seq, j]
        return (pltpu.make_async_copy(ckv_hbm.at[pg, layer], cbuf.at[slot, pl.ds(j * page, page), :],
                                      sem.at[0, slot]),
                pltpu.make_async_copy(kpe_hbm.at[pg, layer], kbuf.at[slot, pl.ds(j * page, page), :],
                                      sem.at[1, slot]))

    def start_seq(seq, slot):
        def body(j, c):
            for cp in copies(seq, slot, j):
                cp.start()
            return c
        lax.fori_loop(0, n_pages, body, 0)

    def wait_seq(seq, slot):
        def body(j, c):
            for cp in copies(seq, slot, j):
                cp.wait()
            return c
        lax.fori_loop(0, n_pages, body, 0)

    @pl.when(b == 0)
    def _():
        start_seq(0, 0)

    @pl.when(b + 1 < nb)
    def _():
        start_seq(b + 1, (b + 1) % 2)

    slot = b % 2
    wait_seq(b, slot)
    qlat = qlat_ref[0]
    qpe = qpe_ref[0]
    ckv = cbuf[slot].astype(BF16)
    kpe = kbuf[slot].astype(BF16)
    s = _dot_nt(qlat.astype(BF16), ckv) + _dot_nt(qpe.astype(BF16), kpe)
    cnew, knew = cnew_ref[0], knew_ref[0]
    s_new = (jnp.sum(qlat.astype(BF16).astype(F32) * cnew.astype(BF16).astype(F32), axis=-1, keepdims=True)
             + jnp.sum(qpe.astype(BF16).astype(F32) * knew.astype(BF16).astype(F32), axis=-1, keepdims=True))
    m = jnp.maximum(jnp.max(s, axis=-1, keepdims=True), s_new)
    p = jnp.exp(s - m)
    p_new = jnp.exp(s_new - m)
    l = jnp.sum(p, axis=-1, keepdims=True) + p_new
    o = _dot(p.astype(BF16), ckv) + p_new.astype(BF16).astype(F32) * cnew.astype(BF16).astype(F32)
    o_ref[0] = o / l


def _attn_decode(page_table, qlat, qpe, cnew, knew, cache_ckv, cache_kpe, layer):
    bsz, n_heads, kv_lora = qlat.shape
    rope_dim = qpe.shape[-1]
    n_pages = page_table.shape[1]
    page = cache_ckv.shape[2]
    per_seq = lambda n, w: pl.BlockSpec((1, n, w), lambda b, pt: (b, 0, 0))
    grid_spec = pltpu.PrefetchScalarGridSpec(
        num_scalar_prefetch=1,
        grid=(bsz,),
        in_specs=[per_seq(n_heads, kv_lora), per_seq(n_heads, rope_dim), per_seq(1, kv_lora),
                  per_seq(1, rope_dim), pl.BlockSpec(memory_space=pl.ANY),
                  pl.BlockSpec(memory_space=pl.ANY)],
        out_specs=per_seq(n_heads, kv_lora),
        scratch_shapes=[pltpu.VMEM((2, n_pages * page, kv_lora), F32),
                        pltpu.VMEM((2, n_pages * page, rope_dim), F32),
                        pltpu.SemaphoreType.DMA((2, 2))],
    )
    return pl.pallas_call(
        functools.partial(_attn_decode_kernel, n_pages=n_pages, page=page, layer=layer),
        grid_spec=grid_spec,
        out_shape=jax.ShapeDtypeStruct((bsz, n_heads, kv_lora), F32),
        compiler_params=_cp("arbitrary"),
        name="attn_decode",
    )(page_table, qlat, qpe, cnew, knew, cache_ckv, cache_kpe)


def _decode_out_kernel(h_ref, olat_ref, wuv_ref, wo_ref, y_ref, o_scr, *, n_pairs, kv_lora):
    for p in range(n_pairs):
        blk = olat_ref[:, 2 * p * kv_lora:(2 * p + 2) * kv_lora].astype(BF16)
        o_scr[:, p * LANES:(p + 1) * LANES] = _dot(blk, wuv_ref[p]).astype(BF16)
    y_ref[...] = h_ref[...] + _dot(o_scr[...], wo_ref[...])


def _decode_out(h, olat, wuv_bd_bf, wo_bf):
    m, d = h.shape
    n_pairs = wuv_bd_bf.shape[0]
    kv_lora = wuv_bd_bf.shape[1] // 2
    return pl.pallas_call(
        functools.partial(_decode_out_kernel, n_pairs=n_pairs, kv_lora=kv_lora),
        out_shape=jax.ShapeDtypeStruct((m, d), F32),
        scratch_shapes=[pltpu.VMEM((m, n_pairs * LANES), BF16)],
        compiler_params=pltpu.CompilerParams(vmem_limit_bytes=VMEM_LIMIT),
        name="decode_out_proj",
    )(h, olat, wuv_bd_bf, wo_bf)


def kernel(x_prompt, x_sample, p_prompt, p_sample, state_conv, cache_ckv, cache_kpe, page_table, norm_mix, norm_ffn, norm_final, conv_w_pw1, conv_b_pw1, conv_w_dw, conv_b_dw, conv_ln_g, conv_ln_b, conv_w_pw2, mla_w_dq, mla_g_q, mla_w_uq, mla_w_dkv, mla_g_kv, mla_w_uk, mla_w_uv, mla_w_o, moe_w_group, moe_b_group, moe_w_expert, moe_b_expert, moe_w_gate, moe_w_up, moe_w_down, ple_w_in, ple_g, ple_w_gate):
    bsz, t, d = x_prompt.shape
    dbsz, dt, _ = x_sample.shape
    depth = p_prompt.shape[0]
    assert depth == 2 and dt == 1
    kv_lora, n_heads, nope = mla_w_uk.shape[1:]
    rope_dim = mla_w_dkv.shape[2] - kv_lora
    vdim = mla_w_uv.shape[3]
    past_len = page_table.shape[1] * cache_ckv.shape[2]
    scale = 1.0 / math.sqrt(nope + rope_dim)
    width = conv_w_dw.shape[1]
    bf = lambda a: a.astype(BF16)

    mp, ms = bsz * t, dbsz * dt
    hp = x_prompt.reshape(mp, d)
    hs = x_sample.reshape(ms, d)

    w1, w2 = bf(conv_w_pw1[0]), bf(conv_w_pw2[0])
    conv_args = (conv_w_dw[0], conv_b_dw[0], conv_ln_g[0], conv_ln_b[0], w2)
    glu_p = _pw1_glu(hp, norm_mix[0], w1, conv_b_pw1[0]).reshape(bsz, t, d)
    hp = _conv_prompt(glu_p, hp.reshape(bsz, t, d), *conv_args).reshape(mp, d)
    conv_state_prompt = glu_p[:, t - (width - 1):][None]
    glu_s = _pw1_glu(hs, norm_mix[0], w1, conv_b_pw1[0])
    buf = state_conv[0]
    hs = _conv_sample(jnp.swapaxes(buf, 0, 1), glu_s, hs, *conv_args)
    conv_state_sample = jnp.concatenate([buf[:, 1:], glu_s[:, None, :]], axis=1)[None]

    def moe_ple(h, p, i, final_norm):
        return _moe_ple_layer(h, p, norm_ffn[i], moe_w_group[i], moe_b_group[i], moe_w_expert[i],
                              moe_b_expert[i], moe_bf[i][0], moe_bf[i][1], moe_bf[i][2],
                              bf(ple_w_in[i]), ple_g[i], bf(ple_w_gate[i]), norm_final, final_norm)

    moe_bf = [(bf(moe_w_gate[i]), bf(moe_w_up[i]), bf(moe_w_down[i])) for i in range(depth)]
    hp = moe_ple(hp, p_prompt[0].reshape(mp, -1), 0, False)
    hs = moe_ple(hs, p_sample[0].reshape(ms, -1), 0, False)

    wts = _mla_weights(mla_w_dq[0], mla_w_uq[0], mla_w_dkv[0], mla_w_uk[0], mla_w_uv[0],
                       n_heads, nope, rope_dim, kv_lora)
    wo = bf(mla_w_o[0])
    half = rope_dim // 2
    natural = lambda kr: jnp.concatenate([kr[:, :half], kr[:, LANES // 2:LANES // 2 + half]], axis=1)

    cos_p, sin_p = _rope_tables(jnp.arange(t, dtype=jnp.int32), rope_dim)
    qn, qr, ckv_p, kr_p, kn, v, krb = _mla_proj(hp, norm_mix[1], wts, mla_g_q[0], mla_g_kv[0],
                                                cos_p, sin_p, True, scale, True, BF16)
    o_p = _attn_prompt(qn, qr, kn, krb, v, bsz, t)
    hp = _out_proj(hp, o_p.reshape(mp, -1), wo)
    ckv_prompt = ckv_p.reshape(1, bsz, t, kv_lora)
    kpe_prompt = natural(kr_p).reshape(1, bsz, t, rope_dim)

    cos_s, sin_s = _rope_tables(past_len + jnp.arange(dt, dtype=jnp.int32), rope_dim)
    qn_s, qr_s, ckv_s, kr_s = _mla_proj(hs, norm_mix[1], wts, mla_g_q[0], mla_g_kv[0],
                                        cos_s, sin_s, False, scale, False, F32)
    n_pairs = n_heads // 2
    wukt = bf(jnp.transpose(mla_w_uk[0], (1, 2, 0)).reshape(n_pairs, 2 * nope, kv_lora))
    qlat = _q_absorb(qn_s, wukt).reshape(ms, n_heads, kv_lora)
    qr3 = qr_s.reshape(ms, n_pairs, LANES)
    x1 = qr3[:, :, :2 * half].reshape(ms, n_heads, half)
    x2 = qr3[:, :, LANES // 2:LANES // 2 + 2 * half].reshape(ms, n_heads, half)
    qpe = jnp.concatenate([x1, x2], axis=-1)
    kpe_s = natural(kr_s)
    olat = _attn_decode(page_table, qlat, qpe, ckv_s.reshape(ms, 1, kv_lora),
                        kpe_s.reshape(ms, 1, rope_dim), cache_ckv, cache_kpe, 0)
    wuv = mla_w_uv[0]
    zv = jnp.zeros((kv_lora, vdim), F32)
    wuv_bd = jnp.stack([jnp.concatenate([jnp.concatenate([wuv[:, 2 * p], zv], axis=1),
                                         jnp.concatenate([zv, wuv[:, 2 * p + 1]], axis=1)], axis=0)
                        for p in range(n_pairs)])
    hs = _decode_out(hs, olat.reshape(ms, n_heads * kv_lora), bf(wuv_bd), wo)
    ckv_sample = ckv_s.reshape(1, dbsz, dt, kv_lora)
    kpe_sample = kpe_s.reshape(1, dbsz, dt, rope_dim)

    y_prompt = moe_ple(hp, p_prompt[1].reshape(mp, -1), 1, True).reshape(bsz, t, d)
    y_sample = moe_ple(hs, p_sample[1].reshape(ms, -1), 1, True).reshape(dbsz, dt, d)
    return (y_prompt, y_sample, conv_state_prompt, conv_state_sample,
            ckv_prompt, kpe_prompt, ckv_sample, kpe_sample)
```

```python
import functools
import math

import jax
import jax.numpy as jnp
import numpy as np
from jax import lax
from jax.experimental import pallas as pl
from jax.experimental.pallas import tpu as pltpu

F32 = jnp.float32
BF16 = jnp.bfloat16

EPS = 1e-6
ROPE_THETA = 10000.0
CONV_HALO = 32
LANES = 128
VMEM_LIMIT = 56 * 1024 * 1024


def _cp(*sem):
    return pltpu.CompilerParams(dimension_semantics=sem, vmem_limit_bytes=VMEM_LIMIT)


def _rms(x, g):
    return x * lax.rsqrt(jnp.mean(x * x, axis=-1, keepdims=True) + EPS) * g


def _dot(a, b):
    return jnp.dot(a, b, preferred_element_type=F32)


def _dot_nt(a, b):
    return lax.dot_general(a, b, (((1,), (1,)), ((), ())), preferred_element_type=F32)


def _row_tile(m, target):
    t = min(m, target)
    while m % t:
        t //= 2
    return t


def _pw1_glu_kernel(x_ref, g_ref, w_ref, b_ref, o_ref):
    d = o_ref.shape[-1]
    hn = _rms(x_ref[...], g_ref[...]).astype(BF16)
    a = _dot(hn, w_ref[:, :d]) + b_ref[:, :d]
    b = _dot(hn, w_ref[:, d:]) + b_ref[:, d:]
    o_ref[...] = a * jax.nn.sigmoid(b)


def _pw1_glu(x, g, w_bf, b):
    m, d = x.shape
    tm = _row_tile(m, 512)
    return pl.pallas_call(
        _pw1_glu_kernel,
        grid=(m // tm,),
        in_specs=[
            pl.BlockSpec((tm, d), lambda i: (i, 0)),
            pl.BlockSpec((1, d), lambda i: (0, 0)),
            pl.BlockSpec((d, 2 * d), lambda i: (0, 0)),
            pl.BlockSpec((1, 2 * d), lambda i: (0, 0)),
        ],
        out_specs=pl.BlockSpec((tm, d), lambda i: (i, 0)),
        out_shape=jax.ShapeDtypeStruct((m, d), F32),
        compiler_params=_cp("parallel"),
        name="pw1_glu",
    )(x, g.reshape(1, d), w_bf, b.reshape(1, 2 * d))


def _ln_silu(z, g, b):
    mu = jnp.mean(z, axis=-1, keepdims=True)
    zc = z - mu
    var = jnp.mean(zc * zc, axis=-1, keepdims=True)
    y = zc * lax.rsqrt(var + EPS) * g + b
    return y * jax.nn.sigmoid(y)


def _conv_prompt_kernel(cur_ref, halo_ref, h_ref, wdw_ref, bdw_ref, lng_ref, lnb_ref, w2_ref,
                        o_ref, full_ref, sh_ref, z_ref, *, width, rows, lanes):
    tq, d = cur_ref.shape[1], cur_ref.shape[2]
    i = pl.program_id(1)
    full_ref[0:CONV_HALO, :] = jnp.where(i > 0, halo_ref[0], 0.0)
    full_ref[CONV_HALO:CONV_HALO + tq, :] = cur_ref[0]
    off = CONV_HALO - (width - 1)
    sub = 8
    for l0 in range(0, d, lanes):
        taps = [len(range(r, width, sub)) for r in range(sub)]
        for r in range(sub):
            n = tq + sub * (taps[r] - 1)
            sh_ref[r, 0:n, :] = full_ref[off + r:off + r + n, l0:l0 + lanes]
        for r0 in range(0, tq, rows):
            acc = jnp.zeros((rows, lanes), F32)
            for k in range(width):
                a, r = divmod(k, sub)
                acc = acc + sh_ref[r, r0 + sub * a:r0 + sub * a + rows, :] * wdw_ref[k:k + 1, l0:l0 + lanes]
            z_ref[r0:r0 + rows, l0:l0 + lanes] = acc
    y = _ln_silu(z_ref[...] + bdw_ref[...], lng_ref[...], lnb_ref[...])
    o_ref[0] = h_ref[0] + _dot(y.astype(BF16), w2_ref[...])


def _conv_prompt(glu, h, w_dw, b_dw, ln_g, ln_b, w2_bf):
    bsz, t, d = glu.shape
    width = w_dw.shape[0]
    tq = _row_tile(t, 512)
    hb = tq // CONV_HALO
    wpad = jnp.zeros((CONV_HALO, d), F32).at[:width].set(w_dw)
    lanes = 256
    kern = functools.partial(_conv_prompt_kernel, width=width, rows=min(64, tq), lanes=lanes)
    vec = lambda a: a.reshape(1, d)
    cvec = pl.BlockSpec((1, d), lambda b, i: (0, 0))
    return pl.pallas_call(
        kern,
        grid=(bsz, t // tq),
        in_specs=[
            pl.BlockSpec((1, tq, d), lambda b, i: (b, i, 0)),
            pl.BlockSpec((1, CONV_HALO, d), lambda b, i: (b, jnp.maximum(i * hb - 1, 0), 0)),
            pl.BlockSpec((1, tq, d), lambda b, i: (b, i, 0)),
            pl.BlockSpec((CONV_HALO, d), lambda b, i: (0, 0)),
            cvec, cvec, cvec,
            pl.BlockSpec((d, d), lambda b, i: (0, 0)),
        ],
        out_specs=pl.BlockSpec((1, tq, d), lambda b, i: (b, i, 0)),
        out_shape=jax.ShapeDtypeStruct((bsz, t, d), F32),
        scratch_shapes=[pltpu.VMEM((tq + CONV_HALO, d), F32),
                        pltpu.VMEM((8, tq + CONV_HALO, lanes), F32), pltpu.VMEM((tq, d), F32)],
        compiler_params=_cp("parallel", "arbitrary"),
        name="conv_prompt",
    )(glu, glu, h, wpad, vec(b_dw), vec(ln_g), vec(ln_b), w2_bf)


def _conv_sample_kernel(buf_ref, u_ref, h_ref, wdw_ref, bdw_ref, lng_ref, lnb_ref, w2_ref, o_ref,
                        *, width):
    acc = u_ref[...] * wdw_ref[width - 1:width, :]
    for k in range(width - 1):
        acc = acc + buf_ref[k] * wdw_ref[k:k + 1, :]
    y = _ln_silu(acc + bdw_ref[...], lng_ref[...], lnb_ref[...])
    o_ref[...] = h_ref[...] + _dot(y.astype(BF16), w2_ref[...])


def _conv_sample(buf_t, u, h, w_dw, b_dw, ln_g, ln_b, w2_bf):
    nb, bsz, d = buf_t.shape
    width = w_dw.shape[0]
    bb = _row_tile(bsz, 32)
    wpad = jnp.zeros((CONV_HALO, d), F32).at[:width].set(w_dw)
    vec = lambda a: a.reshape(1, d)
    cvec = pl.BlockSpec((1, d), lambda i: (0, 0))
    row = pl.BlockSpec((bb, d), lambda i: (i, 0))
    return pl.pallas_call(
        functools.partial(_conv_sample_kernel, width=width),
        grid=(bsz // bb,),
        in_specs=[
            pl.BlockSpec((nb, bb, d), lambda i: (0, i, 0)),
            row, row,
            pl.BlockSpec((CONV_HALO, d), lambda i: (0, 0)),
            cvec, cvec, cvec,
            pl.BlockSpec((d, d), lambda i: (0, 0)),
        ],
        out_specs=row,
        out_shape=jax.ShapeDtypeStruct((bsz, d), F32),
        compiler_params=_cp("parallel"),
        name="conv_sample",
    )(buf_t, u, h, wpad, vec(b_dw), vec(ln_g), vec(ln_b), w2_bf)


def _router_kernel(x_ref, g_ref, w_ref, b_ref, o_ref, cnt_ref, *, n_groups, per_group):
    hn = _rms(x_ref[...], g_ref[...])
    logits = jnp.dot(hn, w_ref[...], preferred_element_type=F32,
                     precision=lax.Precision.HIGHEST) + b_ref[...]
    lane = lax.broadcasted_iota(jnp.int32, logits.shape, 1)
    neg = jnp.float32(-jnp.inf)
    big = jnp.int32(1 << 20)

    def first_argmax(v):
        m = jnp.max(v, axis=-1, keepdims=True)
        return m, jnp.min(jnp.where(v == m, lane, big), axis=-1, keepdims=True)

    gl = jnp.where(lane < n_groups, logits, neg)
    gmax, grp = first_argmax(gl)
    g_w = 1.0 / jnp.sum(jnp.exp(gl - gmax), axis=-1, keepdims=True)
    lo = n_groups + grp * per_group
    el = jnp.where((lane >= lo) & (lane < lo + per_group), logits, neg)
    m1, i1 = first_argmax(el)
    el2 = jnp.where(lane == i1, neg, el)
    m2, i2 = first_argmax(el2)
    e2 = jnp.exp(m2 - m1)
    w1 = g_w / (1.0 + e2)
    w2 = g_w * e2 / (1.0 + e2)
    id1 = (i1 - n_groups).astype(F32)
    id2 = (i2 - n_groups).astype(F32)

    @pl.when(pl.program_id(0) == 0)
    def _():
        cnt_ref[...] = jnp.zeros_like(cnt_ref)

    tm = logits.shape[0]
    picks = ((lane == i1) | (lane == i2)).astype(BF16)
    r_io = lax.broadcasted_iota(jnp.int32, (tm, tm), 0)
    c_io = lax.broadcasted_iota(jnp.int32, (tm, tm), 1)
    before = _dot((c_io < r_io).astype(BF16), picks) + cnt_ref[...]
    rank1 = jnp.sum(jnp.where(lane == i1, before, 0.0), axis=-1, keepdims=True)
    rank2 = jnp.sum(jnp.where(lane == i2, before, 0.0), axis=-1, keepdims=True)
    cnt_ref[...] = cnt_ref[...] + jnp.sum(picks.astype(F32), axis=0, keepdims=True)
    vals = (id1, id2, w1, w2, rank1, rank2)
    out = jnp.zeros_like(logits)
    for k, v in enumerate(vals):
        out = jnp.where(lane == k, v, out)
    o_ref[...] = out


def _router(h, g, w_group, b_group, w_expert, b_expert):
    m, d = h.shape
    n_groups, n_experts = w_group.shape[1], w_expert.shape[1]
    tm = _row_tile(m, 512)
    w = jnp.zeros((d, LANES), F32).at[:, :n_groups].set(w_group)
    w = w.at[:, n_groups:n_groups + n_experts].set(w_expert)
    b = jnp.zeros((1, LANES), F32).at[0, :n_groups].set(b_group)
    b = b.at[0, n_groups:n_groups + n_experts].set(b_expert)
    kern = functools.partial(_router_kernel, n_groups=n_groups, per_group=n_experts // n_groups)
    route, counts = pl.pallas_call(
        kern,
        grid=(m // tm,),
        in_specs=[
            pl.BlockSpec((tm, d), lambda i: (i, 0)),
            pl.BlockSpec((1, d), lambda i: (0, 0)),
            pl.BlockSpec((d, LANES), lambda i: (0, 0)),
            pl.BlockSpec((1, LANES), lambda i: (0, 0)),
        ],
        out_specs=[pl.BlockSpec((tm, LANES), lambda i: (i, 0)), pl.BlockSpec((1, LANES), lambda i: (0, 0))],
        out_shape=[jax.ShapeDtypeStruct((m, LANES), F32), jax.ShapeDtypeStruct((1, LANES), F32)],
        compiler_params=_cp("arbitrary"),
        name="moe_router",
    )(h, g.reshape(1, d), w, b)
    return route, counts[0, n_groups:n_groups + n_experts].astype(jnp.int32)


def _route_plan(route, counts, tm):
    m = route.shape[0]
    n_experts = counts.shape[0]
    experts = jnp.arange(n_experts, dtype=jnp.int32)
    tiles_per = (counts + tm - 1) // tm
    tile_end = jnp.cumsum(tiles_per)
    start = (tile_end - tiles_per) * tm
    eid = route[:, :2].astype(jnp.int32).T
    rank = route[:, 4:6].astype(jnp.int32).T
    pos = rank + jnp.sum(jnp.where(eid[..., None] == experts, start, 0), axis=-1)
    n_tiles = (2 * m) // tm + n_experts
    tile_ids = jnp.arange(n_tiles, dtype=jnp.int32)
    tile_expert = jnp.minimum(jnp.sum((tile_end[None, :] <= tile_ids[:, None]).astype(jnp.int32), axis=1),
                              n_experts - 1)
    n_used = tile_end[-1:].astype(jnp.int32)
    pad_start = (start + counts).astype(jnp.int32)
    pad_len = (tiles_per * tm - counts).astype(jnp.int32)
    pad_len = pad_len.at[n_experts - 1].add((n_tiles - tile_end[-1]).astype(jnp.int32) * tm)
    return pos.reshape(-1).astype(jnp.int32), tile_expert.astype(jnp.int32), n_used, pad_start, pad_len


def _moe_dispatch_kernel(pos_ref, ps_ref, pn_ref, h_hbm, xs_hbm, zrow, sem, *, td, m, n_experts):
    i = pl.program_id(0)

    def row_copy(k, r):
        t = i * td + r
        return pltpu.make_async_copy(h_hbm.at[pl.ds(t, 1), :],
                                     xs_hbm.at[pl.ds(pos_ref[k * m + t], 1), :], sem.at[0])

    def pad_copy(e, j):
        return pltpu.make_async_copy(zrow, xs_hbm.at[pl.ds(ps_ref[e] + j, 1), :], sem.at[1])

    def for_rows(fn):
        def body(r, c):
            fn(row_copy(0, r))
            fn(row_copy(1, r))
            return c
        lax.fori_loop(0, td, body, 0, unroll=8)

    def for_pads(fn):
        def per_expert(e, c):
            lax.fori_loop(0, pn_ref[e], lambda j, c2: (fn(pad_copy(e, j)), c2)[1], 0)
            return c
        lax.fori_loop(0, n_experts, per_expert, 0)

    for_rows(lambda cp: cp.start())

    @pl.when(i == 0)
    def _():
        zrow[...] = jnp.zeros_like(zrow)
        for_pads(lambda cp: cp.start())
        for_pads(lambda cp: cp.wait())

    for_rows(lambda cp: cp.wait())


def _moe_dispatch(h, pos, pad_start, pad_len, n_rows):
    m, d = h.shape
    td = _row_tile(m, 512)
    n_experts = pad_start.shape[0]
    grid_spec = pltpu.PrefetchScalarGridSpec(
        num_scalar_prefetch=3,
        grid=(m // td,),
        in_specs=[pl.BlockSpec(memory_space=pl.ANY)],
        out_specs=pl.BlockSpec(memory_space=pl.ANY),
        scratch_shapes=[pltpu.VMEM((1, d), F32), pltpu.SemaphoreType.DMA((2,))],
    )
    return pl.pallas_call(
        functools.partial(_moe_dispatch_kernel, td=td, m=m, n_experts=n_experts),
        grid_spec=grid_spec,
        out_shape=jax.ShapeDtypeStruct((n_rows, d), F32),
        compiler_params=_cp("arbitrary"),
        name="moe_dispatch",
    )(pos, pad_start, pad_len, h)


def _moe_expert_kernel(te_ref, nu_ref, x_ref, g_ref, wg_ref, wu_ref, wd_ref, o_ref, wgb, wub, wdb):
    i = pl.program_id(0)

    @pl.when((i == 0) | (te_ref[i] != te_ref[jnp.maximum(i - 1, 0)]))
    def _():
        wgb[...] = wg_ref[0, 0].astype(BF16)
        wub[...] = wu_ref[0, 0].astype(BF16)
        wdb[...] = wd_ref[0, 0].astype(BF16)

    @pl.when(i < nu_ref[0])
    def _():
        x = _rms(x_ref[...], g_ref[...]).astype(BF16)
        a = _dot(x, wgb[...])
        u = _dot(x, wub[...])
        act = (a * jax.nn.sigmoid(a) * u).astype(BF16)
        o_ref[...] = _dot(act, wdb[...])

    @pl.when(i >= nu_ref[0])
    def _():
        o_ref[...] = jnp.zeros_like(o_ref)


def _moe_experts(xs, g, tile_expert, n_used, w_gate, w_up, w_down, layer, tm):
    n_rows, d = xs.shape
    n_tiles = n_rows // tm
    f = w_gate.shape[3]
    xmap = lambda i, te, nu: (jnp.minimum(i, nu[0] - 1), 0)
    grid_spec = pltpu.PrefetchScalarGridSpec(
        num_scalar_prefetch=2,
        grid=(n_tiles,),
        in_specs=[
            pl.BlockSpec((tm, d), xmap),
            pl.BlockSpec((1, d), lambda i, te, nu: (0, 0)),
            pl.BlockSpec((1, 1, d, f), lambda i, te, nu: (layer, te[i], 0, 0)),
            pl.BlockSpec((1, 1, d, f), lambda i, te, nu: (layer, te[i], 0, 0)),
            pl.BlockSpec((1, 1, f, d), lambda i, te, nu: (layer, te[i], 0, 0)),
        ],
        out_specs=pl.BlockSpec((tm, d), lambda i, te, nu: (i, 0)),
        scratch_shapes=[pltpu.VMEM((d, f), BF16), pltpu.VMEM((d, f), BF16), pltpu.VMEM((f, d), BF16)],
    )
    return pl.pallas_call(
        _moe_expert_kernel,
        grid_spec=grid_spec,
        out_shape=jax.ShapeDtypeStruct((n_rows, d), F32),
        compiler_params=_cp("arbitrary"),
        name="moe_experts",
    )(tile_expert, n_used, xs, g.reshape(1, d), w_gate, w_up, w_down)


def _combine_ple_kernel(pos_ref, h_ref, route_ref, y_hbm, p_ref, win_ref, g_ref, wgate_ref,
                        gfin_ref, o_ref, ybuf, sem, *, tm, final_norm):
    i = pl.program_id(0)
    n = pl.num_programs(0)
    m = n * tm

    def row_copy(tile, slot, k, r):
        return pltpu.make_async_copy(y_hbm.at[pl.ds(pos_ref[k * m + tile * tm + r], 1), :],
                                     ybuf.at[slot, pl.ds(k * tm + r, 1), :], sem.at[slot])

    def start_tile(tile, slot):
        def body(r, c):
            row_copy(tile, slot, 0, r).start()
            row_copy(tile, slot, 1, r).start()
            return c
        lax.fori_loop(0, tm, body, 0, unroll=8)

    def wait_tile(tile, slot):
        def body(r, c):
            row_copy(tile, slot, 0, r).wait()
            row_copy(tile, slot, 1, r).wait()
            return c
        lax.fori_loop(0, tm, body, 0, unroll=8)

    @pl.when(i == 0)
    def _():
        start_tile(0, 0)

    @pl.when(i + 1 < n)
    def _():
        start_tile(i + 1, (i + 1) % 2)

    slot = i % 2
    wait_tile(i, slot)
    route = route_ref[...]
    h2 = h_ref[...] + route[:, 2:3] * ybuf[slot, 0:tm, :] + route[:, 3:4] * ybuf[slot, tm:2 * tm, :]
    gate = jax.nn.sigmoid(_dot(_rms(h2, g_ref[...]).astype(BF16), wgate_ref[...]))
    h3 = h2 + _dot(p_ref[...].astype(BF16), win_ref[...]) * gate
    if final_norm:
        h3 = _rms(h3, gfin_ref[...])
    o_ref[...] = h3


def _combine_ple(h, route, pos, y_sorted, p, win_bf, g_ple, wgate_bf, g_final, tm, final_norm):
    m, d = h.shape
    pd = p.shape[1]
    grid_spec = pltpu.PrefetchScalarGridSpec(
        num_scalar_prefetch=1,
        grid=(m // tm,),
        in_specs=[
            pl.BlockSpec((tm, d), lambda i, pos: (i, 0)),
            pl.BlockSpec((tm, LANES), lambda i, pos: (i, 0)),
            pl.BlockSpec(memory_space=pl.ANY),
            pl.BlockSpec((tm, pd), lambda i, pos: (i, 0)),
            pl.BlockSpec((pd, d), lambda i, pos: (0, 0)),
            pl.BlockSpec((1, d), lambda i, pos: (0, 0)),
            pl.BlockSpec((d, d), lambda i, pos: (0, 0)),
            pl.BlockSpec((1, d), lambda i, pos: (0, 0)),
        ],
        out_specs=pl.BlockSpec((tm, d), lambda i, pos: (i, 0)),
        scratch_shapes=[pltpu.VMEM((2, 2 * tm, d), F32), pltpu.SemaphoreType.DMA((2,))],
    )
    return pl.pallas_call(
        functools.partial(_combine_ple_kernel, tm=tm, final_norm=final_norm),
        grid_spec=grid_spec,
        out_shape=jax.ShapeDtypeStruct((m, d), F32),
        compiler_params=_cp("arbitrary"),
        name="moe_combine_ple",
    )(pos, h, route, y_sorted, p, win_bf, g_ple.reshape(1, d), wgate_bf, g_final.reshape(1, d))


def _moe_ple_layer(h, p, norm_ffn, w_group, b_group, w_expert, b_expert, w_gate, w_up, w_down, layer,
                   win_bf, g_ple, wgate_bf, g_final, final_norm):
    m = h.shape[0]
    n_experts = w_expert.shape[1]
    tm_e = 256 if m >= 4096 else 32
    route, counts = _router(h, norm_ffn, w_group, b_group, w_expert, b_expert)
    pos, tile_expert, n_used, pad_start, pad_len = _route_plan(route, counts, tm_e)
    n_rows = 2 * m + n_experts * tm_e
    xs = _moe_dispatch(h, pos, pad_start, pad_len, n_rows)
    y_sorted = _moe_experts(xs, norm_ffn, tile_expert, n_used, w_gate, w_up, w_down, layer, tm_e)
    tm_c = _row_tile(m, 256)
    return _combine_ple(h, route, pos, y_sorted, p, win_bf, g_ple, wgate_bf, g_final, tm_c, final_norm)


def _rope_block(x, c, s):
    return x * c + pltpu.roll(x, 64, 1) * s


def _mla_proj_kernel(x_ref, gmix_ref, wdq_ref, gq_ref, wuq_ref, wdkv_ref, gkv_ref, cos_ref, sin_ref,
                     qn_ref, qr_ref, ckv_ref, kr_ref, *, scale, kv_lora, n_pairs):
    hn = _rms(x_ref[...], gmix_ref[...]).astype(BF16)
    cq = _rms(_dot(hn, wdq_ref[...]), gq_ref[...]).astype(BF16)
    q = _dot(cq, wuq_ref[...]) * scale
    nn = n_pairs * LANES
    c, s = cos_ref[...], sin_ref[...]
    qn_ref[...] = q[:, :nn]
    for p in range(n_pairs):
        blk = q[:, nn + p * LANES:nn + (p + 1) * LANES]
        qr_ref[:, p * LANES:(p + 1) * LANES] = _rope_block(blk, c, s)
    kv = _dot(hn, wdkv_ref[...])
    ckv_ref[...] = _rms(kv[:, :kv_lora], gkv_ref[...])
    kr_ref[...] = _rope_block(kv[:, kv_lora:], c, s)


def _mla_proj_prompt_kernel(x_ref, gmix_ref, wdq_ref, gq_ref, wqt_ref, wdkv_ref, gkv_ref, cos_ref, sin_ref,
                            cost_ref, sint_ref, wuk_ref, wuvt_ref,
                            qt_ref, ckv_ref, kr_ref, kn_ref, krb_ref, vt_ref, *, scale, kv_lora, n_pairs):
    tm = x_ref.shape[0]
    half = LANES // 2
    hn = _rms(x_ref[...], gmix_ref[...]).astype(BF16)
    cq = _rms(_dot(hn, wdq_ref[...]), gq_ref[...]).astype(BF16)
    q3 = (_dot_nt(wqt_ref[...], cq) * scale).reshape(n_pairs, 2 * LANES, tm)
    rope = q3[:, LANES:, :]
    rolled = jnp.concatenate([rope[:, half:, :], rope[:, :half, :]], axis=1)
    qt_ref[:, :LANES, :] = q3[:, :LANES, :].astype(BF16)
    qt_ref[:, LANES:, :] = (rope * cost_ref[...][None] + rolled * sint_ref[...][None]).astype(BF16)
    kv = _dot(hn, wdkv_ref[...])
    ckv = _rms(kv[:, :kv_lora], gkv_ref[...])
    kr = _rope_block(kv[:, kv_lora:], cos_ref[...], sin_ref[...])
    ckv_ref[...] = ckv
    kr_ref[...] = kr
    cb = ckv.astype(BF16)
    kn_ref[...] = _dot(cb, wuk_ref[...]).astype(BF16)
    krb_ref[...] = kr.astype(BF16)
    vt_ref[...] = _dot_nt(wuvt_ref[...], cb).reshape(n_pairs, LANES, tm).astype(BF16)


def _rope_tables(pos, rope_dim):
    half = rope_dim // 2
    inv_freq = jnp.power(jnp.float32(ROPE_THETA), -jnp.arange(half, dtype=F32) / half)
    ang = pos.astype(F32)[:, None] * inv_freq[None, :]
    cos, sin = jnp.cos(ang), jnp.sin(ang)
    z = jnp.zeros((pos.shape[0], 2 * half), F32)
    c = jnp.concatenate([cos, cos, z, cos, cos, z], axis=1)
    s = jnp.concatenate([-sin, -sin, z, sin, sin, z], axis=1)
    return c, s


def _mla_weights(w_dq, w_uq, w_dkv, w_uk, w_uv, n_heads, nope, rope_dim, kv_lora):
    half = rope_dim // 2
    assert 2 * nope == LANES and 4 * half <= LANES // 2
    n_pairs = n_heads // 2
    hd = nope + rope_dim
    idx_n = np.array([h * hd + n for h in range(n_heads) for n in range(nope)])
    w_qn = w_uq[:, idx_n]
    zq = jnp.zeros((w_uq.shape[0], LANES // 2 - 2 * half), F32)
    blocks = []
    for p in range(n_pairs):
        h0, h1 = 2 * p, 2 * p + 1
        x1 = lambda h: w_uq[:, h * hd + nope:h * hd + nope + half]
        x2 = lambda h: w_uq[:, h * hd + nope + half:h * hd + hd]
        blocks += [x1(h0), x1(h1), zq, x2(h0), x2(h1), zq]
    w_qr = jnp.concatenate(blocks, axis=1)
    w_q = jnp.concatenate([w_qn, w_qr], axis=1).astype(BF16)
    w_qt = jnp.concatenate(
        [jnp.concatenate([w_qn[:, p * LANES:(p + 1) * LANES].T, w_qr[:, p * LANES:(p + 1) * LANES].T], axis=0)
         for p in range(n_pairs)], axis=0).astype(BF16)
    zk = jnp.zeros((w_dkv.shape[0], LANES // 2 - 2 * half), F32)
    k1 = w_dkv[:, kv_lora:kv_lora + half]
    k2 = w_dkv[:, kv_lora + half:]
    w_kv = jnp.concatenate([w_dkv[:, :kv_lora], k1, k1, zk, k2, k2, zk], axis=1).astype(BF16)
    w_k = w_uk.reshape(kv_lora, n_heads * nope).astype(BF16)
    w_vt = w_uv.reshape(kv_lora, -1).T.astype(BF16)
    return w_dq.astype(BF16), w_q, w_kv, w_k, w_qt, w_vt


def _mla_proj(h, g_mix, wts, g_q, g_kv, cos, sin, scale):
    m, d = h.shape
    w_dq, w_q, w_kv, w_k = wts[:4]
    q_lora = w_dq.shape[1]
    kv_lora = w_k.shape[0]
    n_pairs = w_k.shape[1] // LANES
    nn = n_pairs * LANES
    tm = _row_tile(m, 512)
    tab = pl.BlockSpec((1, LANES), lambda i: (0, 0))
    const = lambda a: pl.BlockSpec(a.shape, lambda i: (0,) * a.ndim)
    row = lambda n: pl.BlockSpec((tm, n), lambda i: (i, 0))
    gq2, gkv2, gm2 = g_q.reshape(1, q_lora), g_kv.reshape(1, kv_lora), g_mix.reshape(1, d)
    return pl.pallas_call(
        functools.partial(_mla_proj_kernel, scale=scale, kv_lora=kv_lora, n_pairs=n_pairs),
        grid=(m // tm,),
        in_specs=[row(d), const(gm2), const(w_dq), const(gq2), const(w_q), const(w_kv), const(gkv2),
                  tab, tab],
        out_specs=[row(nn), row(nn), row(kv_lora), row(LANES)],
        out_shape=[jax.ShapeDtypeStruct((m, nn), F32), jax.ShapeDtypeStruct((m, nn), F32),
                   jax.ShapeDtypeStruct((m, kv_lora), F32), jax.ShapeDtypeStruct((m, LANES), F32)],
        compiler_params=_cp("parallel"),
        name="mla_proj",
    )(h, gm2, w_dq, gq2, w_q, w_kv, gkv2, cos, sin)


def _mla_proj_prompt(h, g_mix, wts, g_q, g_kv, cos, sin, scale):
    m, d = h.shape
    w_dq, _, w_kv, w_k, w_qt, w_vt = wts
    q_lora = w_dq.shape[1]
    kv_lora = w_k.shape[0]
    n_pairs = w_k.shape[1] // LANES
    nn = n_pairs * LANES
    t = cos.shape[0]
    tm = _row_tile(t, 512)
    nt = t // tm
    tab = pl.BlockSpec((tm, LANES), lambda i: (i % nt, 0))
    tab_t = pl.BlockSpec((LANES, tm), lambda i: (0, i % nt))
    const = lambda a: pl.BlockSpec(a.shape, lambda i: (0,) * a.ndim)
    row = lambda n: pl.BlockSpec((tm, n), lambda i: (i, 0))
    col = lambda n: pl.BlockSpec((n_pairs, n, tm), lambda i: (0, 0, i))
    gq2, gkv2, gm2 = g_q.reshape(1, q_lora), g_kv.reshape(1, kv_lora), g_mix.reshape(1, d)
    return pl.pallas_call(
        functools.partial(_mla_proj_prompt_kernel, scale=scale, kv_lora=kv_lora, n_pairs=n_pairs),
        grid=(m // tm,),
        in_specs=[row(d), const(gm2), const(w_dq), const(gq2), const(w_qt), const(w_kv), const(gkv2),
                  tab, tab, tab_t, tab_t, const(w_k), const(w_vt)],
        out_specs=[col(2 * LANES), row(kv_lora), row(LANES), row(nn), row(LANES), col(LANES)],
        out_shape=[jax.ShapeDtypeStruct((n_pairs, 2 * LANES, m), BF16),
                   jax.ShapeDtypeStruct((m, kv_lora), F32), jax.ShapeDtypeStruct((m, LANES), F32),
                   jax.ShapeDtypeStruct((m, nn), BF16), jax.ShapeDtypeStruct((m, LANES), BF16),
                   jax.ShapeDtypeStruct((n_pairs, LANES, m), BF16)],
        compiler_params=_cp("parallel"),
        name="mla_proj_prompt",
    )(h, gm2, w_dq, gq2, w_qt, w_kv, gkv2, cos, sin, cos.T, sin.T, w_k, w_vt)


def _attn_prompt_kernel(qt_ref, kn_ref, kr_ref, vt_ref, o_ref, *, tq):
    i = pl.program_id(2)
    half = LANES // 2
    qt = qt_ref[0].astype(F32)
    row = lax.broadcasted_iota(jnp.int32, qt.shape, 0)
    ws = []
    for hh in range(2):
        sel = (((row < LANES) & ((row // half) == hh))
               | ((row >= LANES) & (((row % half) // 16) == hh)))
        ws.append(jnp.where(sel, qt, 0.0).astype(BF16))

    def block(j, carry, masked):
        r0 = pl.multiple_of(j * tq, tq)
        k = jnp.concatenate([kn_ref[0, pl.ds(r0, tq), :], kr_ref[0, pl.ds(r0, tq), :]], axis=1)
        vt = vt_ref[0, :, pl.ds(r0, tq)]
        out = []
        for hh in range(2):
            m, l, acc = carry[hh]
            st = _dot(k, ws[hh])
            if masked:
                key = lax.broadcasted_iota(jnp.int32, st.shape, 0)
                qry = lax.broadcasted_iota(jnp.int32, st.shape, 1)
                st = jnp.where(key <= qry, st, -jnp.inf)
            m_new = jnp.maximum(m, jnp.max(st, axis=0, keepdims=True))
            alpha = jnp.exp(m - m_new)
            pt = jnp.exp(st - m_new)
            l = alpha * l + jnp.sum(pt, axis=0, keepdims=True)
            acc = alpha * acc + _dot(vt[hh * half:(hh + 1) * half, :], pt.astype(BF16))
            out.append((m_new, l, acc))
        return tuple(out)

    init = tuple((jnp.full((1, tq), -jnp.inf, F32), jnp.zeros((1, tq), F32), jnp.zeros((half, tq), F32))
                 for _ in range(2))
    carry = lax.fori_loop(0, i, lambda j, c: block(j, c, False), init)
    (_, l0, a0), (_, l1, a1) = block(i, carry, True)
    ot = jnp.concatenate([a0 / l0, a1 / l1], axis=0)
    o_ref[0] = ot.T.astype(o_ref.dtype)


def _attn_prompt(qt, kn, kr, vt, bsz, t):
    n_pairs = qt.shape[0]
    tq = _row_tile(t, 512)
    nq = t // tq
    r3 = lambda a: a.reshape(bsz, t, a.shape[-1])
    return pl.pallas_call(
        functools.partial(_attn_prompt_kernel, tq=tq),
        grid=(bsz, n_pairs, nq),
        in_specs=[pl.BlockSpec((1, 2 * LANES, tq), lambda b, p, i: (p, 0, b * nq + i)),
                  pl.BlockSpec((1, t, LANES), lambda b, p, i: (b, 0, p)),
                  pl.BlockSpec((1, t, LANES), lambda b, p, i: (b, 0, 0)),
                  pl.BlockSpec((1, LANES, t), lambda b, p, i: (p, 0, b))],
        out_specs=pl.BlockSpec((1, tq, LANES), lambda b, p, i: (b, i, p)),
        out_shape=jax.ShapeDtypeStruct((bsz, t, n_pairs * LANES), BF16),
        compiler_params=_cp("parallel", "parallel", "arbitrary"),
        name="attn_prompt",
    )(qt, r3(kn), r3(kr), vt)


def _out_proj_kernel(h_ref, o_ref, w_ref, y_ref):
    y_ref[...] = h_ref[...] + _dot(o_ref[...], w_ref[...])


def _out_proj(h, o_bf, wo_bf):
    m, d = h.shape
    n = o_bf.shape[1]
    tm = _row_tile(m, 512)
    return pl.pallas_call(
        _out_proj_kernel,
        grid=(m // tm,),
        in_specs=[pl.BlockSpec((tm, d), lambda i: (i, 0)), pl.BlockSpec((tm, n), lambda i: (i, 0)),
                  pl.BlockSpec((n, d), lambda i: (0, 0))],
        out_specs=pl.BlockSpec((tm, d), lambda i: (i, 0)),
        out_shape=jax.ShapeDtypeStruct((m, d), F32),
        compiler_params=_cp("parallel"),
        name="attn_out_proj",
    )(h, o_bf, wo_bf)


def _q_absorb_kernel(qn_ref, wukt_ref, o_ref, *, n_pairs, kv_lora):
    lane = lax.broadcasted_iota(jnp.int32, (1, LANES), 1)
    half = LANES // 2
    for p in range(n_pairs):
        blk = qn_ref[:, p * LANES:(p + 1) * LANES]
        for hh in range(2):
            qm = jnp.where((lane // half) == hh, blk, 0.0).astype(BF16)
            h = 2 * p + hh
            o_ref[:, h * kv_lora:(h + 1) * kv_lora] = _dot(qm, wukt_ref[p])


def _q_absorb(qn, wukt_bf):
    m = qn.shape[0]
    n_pairs, _, kv_lora = wukt_bf.shape
    return pl.pallas_call(
        functools.partial(_q_absorb_kernel, n_pairs=n_pairs, kv_lora=kv_lora),
        out_shape=jax.ShapeDtypeStruct((m, 2 * n_pairs * kv_lora), F32),
        compiler_params=pltpu.CompilerParams(vmem_limit_bytes=VMEM_LIMIT),
        name="q_absorb",
    )(qn, wukt_bf)


def _attn_decode_kernel(pt_ref, qlat_ref, qpe_ref, cnew_ref, knew_ref, ckv_hbm, kpe_hbm, o_ref,
                        cbuf, kbuf, sem, *, n_pages, page, layer, n_chunks):
    b = pl.program_id(0)
    nb = pl.num_programs(0)

    def copies(seq, slot, j):
        pg = pt_ref[seq, j]
        return (pltpu.make_async_copy(ckv_hbm.at[pg, layer], cbuf.at[slot, pl.ds(j * page, page), :],
                                      sem.at[0, slot]),
                pltpu.make_async_copy(kpe_hbm.at[pg, layer], kbuf.at[slot, :, pl.ds(j * page, page)],
                                      sem.at[1, slot]))

    def start_seq(seq, slot):
        def body(j, c):
            for cp in copies(seq, slot, j):
                cp.start()
            return c
        lax.fori_loop(0, n_pages, body, 0, unroll=4)

    def wait_seq(seq, slot):
        def body(j, c):
            for cp in copies(seq, slot, j):
                cp.wait()
            return c
        lax.fori_loop(0, n_pages, body, 0, unroll=4)

    @pl.when(b == 0)
    def _():
        start_seq(0, 0)

    @pl.when(b + 1 < nb)
    def _():
        start_seq(b + 1, (b + 1) % 2)

    slot = b % 2
    wait_seq(b, slot)
    ql = qlat_ref[0].astype(BF16)
    qp = qpe_ref[0].astype(BF16)
    cnew = cnew_ref[0].astype(BF16).astype(F32)
    knew = knew_ref[0].astype(BF16).astype(F32)
    s_new = (jnp.sum(ql.astype(F32) * cnew, axis=-1, keepdims=True)
             + jnp.sum(qp.astype(F32) * knew, axis=-1, keepdims=True))
    ch = (n_pages * page) // n_chunks

    def chunk(state, c):
        m, l, acc = state
        ck = cbuf[slot, c * ch:(c + 1) * ch, :].astype(BF16)
        kt = kbuf[slot, :, c * ch:(c + 1) * ch].astype(BF16)
        s = _dot_nt(ql, ck) + _dot(qp, kt)
        m_new = jnp.maximum(m, jnp.max(s, axis=-1, keepdims=True))
        alpha = jnp.exp(m - m_new)
        p = jnp.exp(s - m_new)
        return (m_new, alpha * l + jnp.sum(p, axis=-1, keepdims=True),
                alpha * acc + _dot(p.astype(BF16), ck))

    n_a = (n_chunks + 1) // 2
    st_a = (s_new, jnp.ones_like(s_new), jnp.broadcast_to(cnew, (ql.shape[0], cnew.shape[1])))
    for c in range(n_a):
        st_a = chunk(st_a, c)
    if n_chunks > n_a:
        st_b = (jnp.full_like(s_new, -jnp.inf), jnp.zeros_like(s_new), jnp.zeros_like(st_a[2]))
        for c in range(n_a, n_chunks):
            st_b = chunk(st_b, c)
        m = jnp.maximum(st_a[0], st_b[0])
        fa, fb = jnp.exp(st_a[0] - m), jnp.exp(st_b[0] - m)
        l = fa * st_a[1] + fb * st_b[1]
        acc = fa * st_a[2] + fb * st_b[2]
    else:
        _, l, acc = st_a
    o_ref[0] = acc / l


def _attn_decode(page_table, qlat, qpe, cnew, knew, cache_ckv, cache_kpe_t, layer):
    bsz, n_heads, kv_lora = qlat.shape
    rope_dim = qpe.shape[-1]
    n_pages = page_table.shape[1]
    page = cache_ckv.shape[2]
    n_chunks = 4 if (n_pages * page) % (4 * LANES) == 0 else 1
    per_seq = lambda n, w: pl.BlockSpec((1, n, w), lambda b, pt: (b, 0, 0))
    grid_spec = pltpu.PrefetchScalarGridSpec(
        num_scalar_prefetch=1,
        grid=(bsz,),
        in_specs=[per_seq(n_heads, kv_lora), per_seq(n_heads, rope_dim), per_seq(1, kv_lora),
                  per_seq(1, rope_dim), pl.BlockSpec(memory_space=pl.ANY),
                  pl.BlockSpec(memory_space=pl.ANY)],
        out_specs=per_seq(n_heads, kv_lora),
        scratch_shapes=[pltpu.VMEM((2, n_pages * page, kv_lora), F32),
                        pltpu.VMEM((2, rope_dim, n_pages * page), F32),
                        pltpu.SemaphoreType.DMA((2, 2))],
    )
    return pl.pallas_call(
        functools.partial(_attn_decode_kernel, n_pages=n_pages, page=page, layer=layer,
                          n_chunks=n_chunks),
        grid_spec=grid_spec,
        out_shape=jax.ShapeDtypeStruct((bsz, n_heads, kv_lora), F32),
        compiler_params=_cp("arbitrary"),
        name="attn_decode",
    )(page_table, qlat, qpe, cnew, knew, cache_ckv, cache_kpe_t)


def _decode_out_kernel(h_ref, olat_ref, wuv_ref, wo_ref, y_ref, o_scr, *, n_pairs, kv_lora):
    for p in range(n_pairs):
        blk = olat_ref[:, 2 * p * kv_lora:(2 * p + 2) * kv_lora].astype(BF16)
        o_scr[:, p * LANES:(p + 1) * LANES] = _dot(blk, wuv_ref[p]).astype(BF16)
    y_ref[...] = h_ref[...] + _dot(o_scr[...], wo_ref[...])


def _decode_out(h, olat, wuv_bd_bf, wo_bf):
    m, d = h.shape
    n_pairs = wuv_bd_bf.shape[0]
    kv_lora = wuv_bd_bf.shape[1] // 2
    return pl.pallas_call(
        functools.partial(_decode_out_kernel, n_pairs=n_pairs, kv_lora=kv_lora),
        out_shape=jax.ShapeDtypeStruct((m, d), F32),
        scratch_shapes=[pltpu.VMEM((m, n_pairs * LANES), BF16)],
        compiler_params=pltpu.CompilerParams(vmem_limit_bytes=VMEM_LIMIT),
        name="decode_out_proj",
    )(h, olat, wuv_bd_bf, wo_bf)


def kernel(x_prompt, x_sample, p_prompt, p_sample, state_conv, cache_ckv, cache_kpe, page_table, norm_mix, norm_ffn, norm_final, conv_w_pw1, conv_b_pw1, conv_w_dw, conv_b_dw, conv_ln_g, conv_ln_b, conv_w_pw2, mla_w_dq, mla_g_q, mla_w_uq, mla_w_dkv, mla_g_kv, mla_w_uk, mla_w_uv, mla_w_o, moe_w_group, moe_b_group, moe_w_expert, moe_b_expert, moe_w_gate, moe_w_up, moe_w_down, ple_w_in, ple_g, ple_w_gate):
    bsz, t, d = x_prompt.shape
    dbsz, dt, _ = x_sample.shape
    depth = p_prompt.shape[0]
    assert depth == 2 and dt == 1
    kv_lora, n_heads, nope = mla_w_uk.shape[1:]
    rope_dim = mla_w_dkv.shape[2] - kv_lora
    vdim = mla_w_uv.shape[3]
    past_len = page_table.shape[1] * cache_ckv.shape[2]
    scale = 1.0 / math.sqrt(nope + rope_dim)
    width = conv_w_dw.shape[1]
    bf = lambda a: a.astype(BF16)

    mp, ms = bsz * t, dbsz * dt
    hp = x_prompt.reshape(mp, d)
    hs = x_sample.reshape(ms, d)

    w1, w2 = bf(conv_w_pw1[0]), bf(conv_w_pw2[0])
    conv_args = (conv_w_dw[0], conv_b_dw[0], conv_ln_g[0], conv_ln_b[0], w2)
    glu_p = _pw1_glu(hp, norm_mix[0], w1, conv_b_pw1[0]).reshape(bsz, t, d)
    hp = _conv_prompt(glu_p, hp.reshape(bsz, t, d), *conv_args).reshape(mp, d)
    conv_state_prompt = glu_p[:, t - (width - 1):][None]
    glu_s = _pw1_glu(hs, norm_mix[0], w1, conv_b_pw1[0])
    buf = state_conv[0]
    hs = _conv_sample(jnp.swapaxes(buf, 0, 1), glu_s, hs, *conv_args)
    conv_state_sample = jnp.concatenate([buf[:, 1:], glu_s[:, None, :]], axis=1)[None]

    def moe_ple(h, p, i, final_norm):
        return _moe_ple_layer(h, p, norm_ffn[i], moe_w_group[i], moe_b_group[i], moe_w_expert[i],
                              moe_b_expert[i], moe_w_gate, moe_w_up, moe_w_down, i,
                              bf(ple_w_in[i]), ple_g[i], bf(ple_w_gate[i]), norm_final, final_norm)

    hp = moe_ple(hp, p_prompt[0].reshape(mp, -1), 0, False)
    hs = moe_ple(hs, p_sample[0].reshape(ms, -1), 0, False)

    wts = _mla_weights(mla_w_dq[0], mla_w_uq[0], mla_w_dkv[0], mla_w_uk[0], mla_w_uv[0],
                       n_heads, nope, rope_dim, kv_lora)
    wo = bf(mla_w_o[0])
    half = rope_dim // 2
    natural = lambda kr: jnp.concatenate([kr[:, :half], kr[:, LANES // 2:LANES // 2 + half]], axis=1)

    cos_p, sin_p = _rope_tables(jnp.arange(t, dtype=jnp.int32), rope_dim)
    qt, ckv_p, kr_p, kn, krb, vt = _mla_proj_prompt(hp, norm_mix[1], wts, mla_g_q[0], mla_g_kv[0],
                                                    cos_p, sin_p, scale)
    o_p = _attn_prompt(qt, kn, krb, vt, bsz, t)
    hp = _out_proj(hp, o_p.reshape(mp, -1), wo)
    ckv_prompt = ckv_p.reshape(1, bsz, t, kv_lora)
    kpe_prompt = natural(kr_p).reshape(1, bsz, t, rope_dim)

    cos_s, sin_s = _rope_tables(past_len + jnp.arange(dt, dtype=jnp.int32), rope_dim)
    qn_s, qr_s, ckv_s, kr_s = _mla_proj(hs, norm_mix[1], wts, mla_g_q[0], mla_g_kv[0],
                                        cos_s, sin_s, scale)
    n_pairs = n_heads // 2
    wukt = bf(jnp.transpose(mla_w_uk[0], (1, 2, 0)).reshape(n_pairs, 2 * nope, kv_lora))
    qlat = _q_absorb(qn_s, wukt).reshape(ms, n_heads, kv_lora)
    qr3 = qr_s.reshape(ms, n_pairs, LANES)
    x1 = qr3[:, :, :2 * half].reshape(ms, n_heads, half)
    x2 = qr3[:, :, LANES // 2:LANES // 2 + 2 * half].reshape(ms, n_heads, half)
    qpe = jnp.concatenate([x1, x2], axis=-1)
    kpe_s = natural(kr_s)
    olat = _attn_decode(page_table, qlat, qpe, ckv_s.reshape(ms, 1, kv_lora),
                        kpe_s.reshape(ms, 1, rope_dim), cache_ckv, jnp.swapaxes(cache_kpe, 2, 3), 0)
    wuv = mla_w_uv[0]
    zv = jnp.zeros((kv_lora, vdim), F32)
    wuv_bd = jnp.stack([jnp.concatenate([jnp.concatenate([wuv[:, 2 * p], zv], axis=1),
                                         jnp.concatenate([zv, wuv[:, 2 * p + 1]], axis=1)], axis=0)
                        for p in range(n_pairs)])
    hs = _decode_out(hs, olat.reshape(ms, n_heads * kv_lora), bf(wuv_bd), wo)
    ckv_sample = ckv_s.reshape(1, dbsz, dt, kv_lora)
    kpe_sample = kpe_s.reshape(1, dbsz, dt, rope_dim)

    y_prompt = moe_ple(hp, p_prompt[1].reshape(mp, -1), 1, True).reshape(bsz, t, d)
    y_sample = moe_ple(hs, p_sample[1].reshape(ms, -1), 1, True).reshape(dbsz, dt, d)
    return (y_prompt, y_sample, conv_state_prompt, conv_state_sample,
            ckv_prompt, kpe_prompt, ckv_sample, kpe_sample)
```

```python
import functools
import math

import jax
import jax.numpy as jnp
import numpy as np
from jax import lax
from jax.experimental import pallas as pl
from jax.experimental.pallas import tpu as pltpu

F32 = jnp.float32
BF16 = jnp.bfloat16

EPS = 1e-6
ROPE_THETA = 10000.0
CONV_HALO = 32
LANES = 128
VMEM_LIMIT = 56 * 1024 * 1024


def _cp(*sem):
    return pltpu.CompilerParams(dimension_semantics=sem, vmem_limit_bytes=VMEM_LIMIT)


def _rms(x, g):
    return x * lax.rsqrt(jnp.mean(x * x, axis=-1, keepdims=True) + EPS) * g


def _dot(a, b):
    return jnp.dot(a, b, preferred_element_type=F32)


def _dot_nt(a, b):
    return lax.dot_general(a, b, (((1,), (1,)), ((), ())), preferred_element_type=F32)


def _row_tile(m, target):
    t = min(m, target)
    while m % t:
        t //= 2
    return t


def _pw1_glu_kernel(x_ref, g_ref, w_ref, b_ref, o_ref):
    d = o_ref.shape[-1]
    hn = _rms(x_ref[...], g_ref[...]).astype(BF16)
    a = _dot(hn, w_ref[:, :d]) + b_ref[:, :d]
    b = _dot(hn, w_ref[:, d:]) + b_ref[:, d:]
    o_ref[...] = a * jax.nn.sigmoid(b)


def _pw1_glu(x, g, w_bf, b):
    m, d = x.shape
    tm = _row_tile(m, 512)
    return pl.pallas_call(
        _pw1_glu_kernel,
        grid=(m // tm,),
        in_specs=[
            pl.BlockSpec((tm, d), lambda i: (i, 0)),
            pl.BlockSpec((1, d), lambda i: (0, 0)),
            pl.BlockSpec((d, 2 * d), lambda i: (0, 0)),
            pl.BlockSpec((1, 2 * d), lambda i: (0, 0)),
        ],
        out_specs=pl.BlockSpec((tm, d), lambda i: (i, 0)),
        out_shape=jax.ShapeDtypeStruct((m, d), F32),
        compiler_params=_cp("parallel"),
        name="pw1_glu",
    )(x, g.reshape(1, d), w_bf, b.reshape(1, 2 * d))


def _ln_silu(z, g, b):
    mu = jnp.mean(z, axis=-1, keepdims=True)
    zc = z - mu
    var = jnp.mean(zc * zc, axis=-1, keepdims=True)
    y = zc * lax.rsqrt(var + EPS) * g + b
    return y * jax.nn.sigmoid(y)


def _conv_prompt_kernel(cur_ref, halo_ref, h_ref, wdw_ref, bdw_ref, lng_ref, lnb_ref, w2_ref,
                        o_ref, full_ref, sh_ref, z_ref, *, width, rows, lanes):
    tq, d = cur_ref.shape[1], cur_ref.shape[2]
    i = pl.program_id(1)
    full_ref[0:CONV_HALO, :] = jnp.where(i > 0, halo_ref[0], 0.0)
    full_ref[CONV_HALO:CONV_HALO + tq, :] = cur_ref[0]
    off = CONV_HALO - (width - 1)
    sub = 8
    for l0 in range(0, d, lanes):
        taps = [len(range(r, width, sub)) for r in range(sub)]
        for r in range(sub):
            n = tq + sub * (taps[r] - 1)
            sh_ref[r, 0:n, :] = full_ref[off + r:off + r + n, l0:l0 + lanes]
        for r0 in range(0, tq, rows):
            acc = jnp.zeros((rows, lanes), F32)
            for k in range(width):
                a, r = divmod(k, sub)
                acc = acc + sh_ref[r, r0 + sub * a:r0 + sub * a + rows, :] * wdw_ref[k:k + 1, l0:l0 + lanes]
            z_ref[r0:r0 + rows, l0:l0 + lanes] = acc
    y = _ln_silu(z_ref[...] + bdw_ref[...], lng_ref[...], lnb_ref[...])
    o_ref[0] = h_ref[0] + _dot(y.astype(BF16), w2_ref[...])


def _conv_prompt(glu, h, w_dw, b_dw, ln_g, ln_b, w2_bf):
    bsz, t, d = glu.shape
    width = w_dw.shape[0]
    tq = _row_tile(t, 512)
    hb = tq // CONV_HALO
    wpad = jnp.zeros((CONV_HALO, d), F32).at[:width].set(w_dw)
    lanes = 256
    kern = functools.partial(_conv_prompt_kernel, width=width, rows=min(64, tq), lanes=lanes)
    vec = lambda a: a.reshape(1, d)
    cvec = pl.BlockSpec((1, d), lambda b, i: (0, 0))
    return pl.pallas_call(
        kern,
        grid=(bsz, t // tq),
        in_specs=[
            pl.BlockSpec((1, tq, d), lambda b, i: (b, i, 0)),
            pl.BlockSpec((1, CONV_HALO, d), lambda b, i: (b, jnp.maximum(i * hb - 1, 0), 0)),
            pl.BlockSpec((1, tq, d), lambda b, i: (b, i, 0)),
            pl.BlockSpec((CONV_HALO, d), lambda b, i: (0, 0)),
            cvec, cvec, cvec,
            pl.BlockSpec((d, d), lambda b, i: (0, 0)),
        ],
        out_specs=pl.BlockSpec((1, tq, d), lambda b, i: (b, i, 0)),
        out_shape=jax.ShapeDtypeStruct((bsz, t, d), F32),
        scratch_shapes=[pltpu.VMEM((tq + CONV_HALO, d), F32),
                        pltpu.VMEM((8, tq + CONV_HALO, lanes), F32), pltpu.VMEM((tq, d), F32)],
        compiler_params=_cp("parallel", "arbitrary"),
        name="conv_prompt",
    )(glu, glu, h, wpad, vec(b_dw), vec(ln_g), vec(ln_b), w2_bf)


def _conv_sample_kernel(buf_ref, u_ref, h_ref, wdw_ref, bdw_ref, lng_ref, lnb_ref, w2_ref, o_ref,
                        *, width):
    acc = u_ref[...] * wdw_ref[width - 1:width, :]
    for k in range(width - 1):
        acc = acc + buf_ref[k] * wdw_ref[k:k + 1, :]
    y = _ln_silu(acc + bdw_ref[...], lng_ref[...], lnb_ref[...])
    o_ref[...] = h_ref[...] + _dot(y.astype(BF16), w2_ref[...])


def _conv_sample(buf_t, u, h, w_dw, b_dw, ln_g, ln_b, w2_bf):
    nb, bsz, d = buf_t.shape
    width = w_dw.shape[0]
    bb = _row_tile(bsz, 32)
    wpad = jnp.zeros((CONV_HALO, d), F32).at[:width].set(w_dw)
    vec = lambda a: a.reshape(1, d)
    cvec = pl.BlockSpec((1, d), lambda i: (0, 0))
    row = pl.BlockSpec((bb, d), lambda i: (i, 0))
    return pl.pallas_call(
        functools.partial(_conv_sample_kernel, width=width),
        grid=(bsz // bb,),
        in_specs=[
            pl.BlockSpec((nb, bb, d), lambda i: (0, i, 0)),
            row, row,
            pl.BlockSpec((CONV_HALO, d), lambda i: (0, 0)),
            cvec, cvec, cvec,
            pl.BlockSpec((d, d), lambda i: (0, 0)),
        ],
        out_specs=row,
        out_shape=jax.ShapeDtypeStruct((bsz, d), F32),
        compiler_params=_cp("parallel"),
        name="conv_sample",
    )(buf_t, u, h, wpad, vec(b_dw), vec(ln_g), vec(ln_b), w2_bf)


def _router_kernel(x_ref, g_ref, whi_ref, wlo_ref, b_ref, o_ref, cnt_ref, *, n_groups, per_group):
    hn = _rms(x_ref[...], g_ref[...])
    hn_hi = hn.astype(BF16)
    hn_lo = (hn - hn_hi.astype(F32)).astype(BF16)
    logits = (_dot(hn_hi, whi_ref[...]) + _dot(hn_lo, whi_ref[...]) + _dot(hn_hi, wlo_ref[...])
              + b_ref[...])
    lane = lax.broadcasted_iota(jnp.int32, logits.shape, 1)
    neg = jnp.float32(-jnp.inf)
    big = jnp.int32(1 << 20)

    def first_argmax(v):
        m = jnp.max(v, axis=-1, keepdims=True)
        return m, jnp.min(jnp.where(v == m, lane, big), axis=-1, keepdims=True)

    gl = jnp.where(lane < n_groups, logits, neg)
    gmax, grp = first_argmax(gl)
    g_w = 1.0 / jnp.sum(jnp.exp(gl - gmax), axis=-1, keepdims=True)
    lo = n_groups + grp * per_group
    el = jnp.where((lane >= lo) & (lane < lo + per_group), logits, neg)
    m1, i1 = first_argmax(el)
    el2 = jnp.where(lane == i1, neg, el)
    m2, i2 = first_argmax(el2)
    e2 = jnp.exp(m2 - m1)
    w1 = g_w / (1.0 + e2)
    w2 = g_w * e2 / (1.0 + e2)
    id1 = (i1 - n_groups).astype(F32)
    id2 = (i2 - n_groups).astype(F32)

    @pl.when(pl.program_id(0) == 0)
    def _():
        cnt_ref[...] = jnp.zeros_like(cnt_ref)

    tm = logits.shape[0]
    picks = ((lane == i1) | (lane == i2)).astype(BF16)
    r_io = lax.broadcasted_iota(jnp.int32, (tm, tm), 0)
    c_io = lax.broadcasted_iota(jnp.int32, (tm, tm), 1)
    before = _dot((c_io < r_io).astype(BF16), picks) + cnt_ref[...]
    rank1 = jnp.sum(jnp.where(lane == i1, before, 0.0), axis=-1, keepdims=True)
    rank2 = jnp.sum(jnp.where(lane == i2, before, 0.0), axis=-1, keepdims=True)
    cnt_ref[...] = cnt_ref[...] + jnp.sum(picks.astype(F32), axis=0, keepdims=True)
    vals = (id1, id2, w1, w2, rank1, rank2)
    out = jnp.zeros_like(logits)
    for k, v in enumerate(vals):
        out = jnp.where(lane == k, v, out)
    o_ref[...] = out


def _router(h, g, w_group, b_group, w_expert, b_expert):
    m, d = h.shape
    n_groups, n_experts = w_group.shape[1], w_expert.shape[1]
    tm = _row_tile(m, 512)
    w = jnp.zeros((d, LANES), F32).at[:, :n_groups].set(w_group)
    w = w.at[:, n_groups:n_groups + n_experts].set(w_expert)
    b = jnp.zeros((1, LANES), F32).at[0, :n_groups].set(b_group)
    b = b.at[0, n_groups:n_groups + n_experts].set(b_expert)
    kern = functools.partial(_router_kernel, n_groups=n_groups, per_group=n_experts // n_groups)
    w_hi = w.astype(BF16)
    w_lo = (w - w_hi.astype(F32)).astype(BF16)
    route, counts = pl.pallas_call(
        kern,
        grid=(m // tm,),
        in_specs=[
            pl.BlockSpec((tm, d), lambda i: (i, 0)),
            pl.BlockSpec((1, d), lambda i: (0, 0)),
            pl.BlockSpec((d, LANES), lambda i: (0, 0)),
            pl.BlockSpec((d, LANES), lambda i: (0, 0)),
            pl.BlockSpec((1, LANES), lambda i: (0, 0)),
        ],
        out_specs=[pl.BlockSpec((tm, LANES), lambda i: (i, 0)), pl.BlockSpec((1, LANES), lambda i: (0, 0))],
        out_shape=[jax.ShapeDtypeStruct((m, LANES), F32), jax.ShapeDtypeStruct((1, LANES), F32)],
        compiler_params=_cp("arbitrary"),
        name="moe_router",
    )(h, g.reshape(1, d), w_hi, w_lo, b)
    return route, counts[0, n_groups:n_groups + n_experts].astype(jnp.int32)


def _route_plan(route, counts, tm):
    m = route.shape[0]
    n_experts = counts.shape[0]
    experts = jnp.arange(n_experts, dtype=jnp.int32)
    tiles_per = (counts + tm - 1) // tm
    tile_end = jnp.cumsum(tiles_per)
    start = (tile_end - tiles_per) * tm
    eid = route[:, :2].astype(jnp.int32).T
    rank = route[:, 4:6].astype(jnp.int32).T
    pos = rank + jnp.sum(jnp.where(eid[..., None] == experts, start, 0), axis=-1)
    n_tiles = (2 * m) // tm + n_experts
    tile_ids = jnp.arange(n_tiles, dtype=jnp.int32)
    tile_expert = jnp.minimum(jnp.sum((tile_end[None, :] <= tile_ids[:, None]).astype(jnp.int32), axis=1),
                              n_experts - 1)
    n_used = tile_end[-1:].astype(jnp.int32)
    pad_start = (start + counts).astype(jnp.int32)
    pad_len = (tiles_per * tm - counts).astype(jnp.int32)
    pad_len = pad_len.at[n_experts - 1].add((n_tiles - tile_end[-1]).astype(jnp.int32) * tm)
    return pos.reshape(-1).astype(jnp.int32), tile_expert.astype(jnp.int32), n_used, pad_start, pad_len


def _moe_dispatch_kernel(pos_ref, ps_ref, pn_ref, h_ref, xs_hbm, zrow, sem, *, td, m, n_experts):
    i = pl.program_id(0)

    def row_copy(k, r):
        return pltpu.make_async_copy(h_ref.at[pl.ds(r, 1), :],
                                     xs_hbm.at[pl.ds(pos_ref[k * m + i * td + r], 1), :], sem.at[0])

    def pad_copy(e, j):
        return pltpu.make_async_copy(zrow, xs_hbm.at[pl.ds(ps_ref[e] + j, 1), :], sem.at[1])

    def for_rows(fn):
        def body(r, c):
            fn(row_copy(0, r))
            fn(row_copy(1, r))
            return c
        lax.fori_loop(0, td, body, 0, unroll=8)

    def for_pads(fn):
        def per_expert(e, c):
            lax.fori_loop(0, pn_ref[e], lambda j, c2: (fn(pad_copy(e, j)), c2)[1], 0)
            return c
        lax.fori_loop(0, n_experts, per_expert, 0)

    for_rows(lambda cp: cp.start())

    @pl.when(i == 0)
    def _():
        zrow[...] = jnp.zeros_like(zrow)
        for_pads(lambda cp: cp.start())
        for_pads(lambda cp: cp.wait())

    for_rows(lambda cp: cp.wait())


def _moe_dispatch(h, pos, pad_start, pad_len, n_rows):
    m, d = h.shape
    td = _row_tile(m, 512)
    n_experts = pad_start.shape[0]
    grid_spec = pltpu.PrefetchScalarGridSpec(
        num_scalar_prefetch=3,
        grid=(m // td,),
        in_specs=[pl.BlockSpec((td, d), lambda i, pos, ps, pn: (i, 0))],
        out_specs=pl.BlockSpec(memory_space=pl.ANY),
        scratch_shapes=[pltpu.VMEM((1, d), F32), pltpu.SemaphoreType.DMA((2,))],
    )
    return pl.pallas_call(
        functools.partial(_moe_dispatch_kernel, td=td, m=m, n_experts=n_experts),
        grid_spec=grid_spec,
        out_shape=jax.ShapeDtypeStruct((n_rows, d), F32),
        compiler_params=_cp("arbitrary"),
        name="moe_dispatch",
    )(pos, pad_start, pad_len, h)


def _moe_expert_kernel(te_ref, nu_ref, x_ref, g_ref, wg_ref, wu_ref, wd_ref, o_ref, wgb, wub, wdb):
    i = pl.program_id(0)

    @pl.when((i == 0) | (te_ref[i] != te_ref[jnp.maximum(i - 1, 0)]))
    def _():
        wgb[...] = wg_ref[0, 0].astype(BF16)
        wub[...] = wu_ref[0, 0].astype(BF16)
        wdb[...] = wd_ref[0, 0].astype(BF16)

    @pl.when(i < nu_ref[0])
    def _():
        x = _rms(x_ref[...], g_ref[...]).astype(BF16)
        a = _dot(x, wgb[...])
        u = _dot(x, wub[...])
        act = (a * jax.nn.sigmoid(a) * u).astype(BF16)
        o_ref[...] = _dot(act, wdb[...])

    @pl.when(i >= nu_ref[0])
    def _():
        o_ref[...] = jnp.zeros_like(o_ref)


def _moe_experts(xs, g, tile_expert, n_used, w_gate, w_up, w_down, layer, tm):
    n_rows, d = xs.shape
    n_tiles = n_rows // tm
    f = w_gate.shape[3]
    xmap = lambda i, te, nu: (jnp.minimum(i, nu[0] - 1), 0)
    grid_spec = pltpu.PrefetchScalarGridSpec(
        num_scalar_prefetch=2,
        grid=(n_tiles,),
        in_specs=[
            pl.BlockSpec((tm, d), xmap),
            pl.BlockSpec((1, d), lambda i, te, nu: (0, 0)),
            pl.BlockSpec((1, 1, d, f), lambda i, te, nu: (layer, te[i], 0, 0)),
            pl.BlockSpec((1, 1, d, f), lambda i, te, nu: (layer, te[i], 0, 0)),
            pl.BlockSpec((1, 1, f, d), lambda i, te, nu: (layer, te[i], 0, 0)),
        ],
        out_specs=pl.BlockSpec((tm, d), lambda i, te, nu: (i, 0)),
        scratch_shapes=[pltpu.VMEM((d, f), BF16), pltpu.VMEM((d, f), BF16), pltpu.VMEM((f, d), BF16)],
    )
    return pl.pallas_call(
        _moe_expert_kernel,
        grid_spec=grid_spec,
        out_shape=jax.ShapeDtypeStruct((n_rows, d), F32),
        compiler_params=_cp("arbitrary"),
        name="moe_experts",
    )(tile_expert, n_used, xs, g.reshape(1, d), w_gate, w_up, w_down)


def _combine_ple_kernel(pos_ref, h_ref, route_ref, y_hbm, p_ref, win_ref, g_ref, wgate_ref,
                        gfin_ref, o_ref, ybuf, sem, *, tm, final_norm):
    i = pl.program_id(0)
    n = pl.num_programs(0)
    m = n * tm

    def row_copy(tile, slot, k, r):
        return pltpu.make_async_copy(y_hbm.at[pl.ds(pos_ref[k * m + tile * tm + r], 1), :],
                                     ybuf.at[slot, pl.ds(k * tm + r, 1), :], sem.at[slot])

    def start_tile(tile, slot):
        def body(r, c):
            row_copy(tile, slot, 0, r).start()
            row_copy(tile, slot, 1, r).start()
            return c
        lax.fori_loop(0, tm, body, 0, unroll=8)

    def wait_tile(tile, slot):
        def body(r, c):
            row_copy(tile, slot, 0, r).wait()
            row_copy(tile, slot, 1, r).wait()
            return c
        lax.fori_loop(0, tm, body, 0, unroll=8)

    @pl.when(i == 0)
    def _():
        start_tile(0, 0)

    @pl.when(i + 1 < n)
    def _():
        start_tile(i + 1, (i + 1) % 2)

    slot = i % 2
    wait_tile(i, slot)
    route = route_ref[...]
    h2 = h_ref[...] + route[:, 2:3] * ybuf[slot, 0:tm, :] + route[:, 3:4] * ybuf[slot, tm:2 * tm, :]
    gate = jax.nn.sigmoid(_dot(_rms(h2, g_ref[...]).astype(BF16), wgate_ref[...]))
    h3 = h2 + _dot(p_ref[...].astype(BF16), win_ref[...]) * gate
    if final_norm:
        h3 = _rms(h3, gfin_ref[...])
    o_ref[...] = h3


def _combine_ple(h, route, pos, y_sorted, p, win_bf, g_ple, wgate_bf, g_final, tm, final_norm):
    m, d = h.shape
    pd = p.shape[1]
    grid_spec = pltpu.PrefetchScalarGridSpec(
        num_scalar_prefetch=1,
        grid=(m // tm,),
        in_specs=[
            pl.BlockSpec((tm, d), lambda i, pos: (i, 0)),
            pl.BlockSpec((tm, LANES), lambda i, pos: (i, 0)),
            pl.BlockSpec(memory_space=pl.ANY),
            pl.BlockSpec((tm, pd), lambda i, pos: (i, 0)),
            pl.BlockSpec((pd, d), lambda i, pos: (0, 0)),
            pl.BlockSpec((1, d), lambda i, pos: (0, 0)),
            pl.BlockSpec((d, d), lambda i, pos: (0, 0)),
            pl.BlockSpec((1, d), lambda i, pos: (0, 0)),
        ],
        out_specs=pl.BlockSpec((tm, d), lambda i, pos: (i, 0)),
        scratch_shapes=[pltpu.VMEM((2, 2 * tm, d), F32), pltpu.SemaphoreType.DMA((2,))],
    )
    return pl.pallas_call(
        functools.partial(_combine_ple_kernel, tm=tm, final_norm=final_norm),
        grid_spec=grid_spec,
        out_shape=jax.ShapeDtypeStruct((m, d), F32),
        compiler_params=_cp("arbitrary"),
        name="moe_combine_ple",
    )(pos, h, route, y_sorted, p, win_bf, g_ple.reshape(1, d), wgate_bf, g_final.reshape(1, d))


def _moe_ple_layer(h, p, norm_ffn, w_group, b_group, w_expert, b_expert, w_gate, w_up, w_down, layer,
                   win_bf, g_ple, wgate_bf, g_final, final_norm):
    m = h.shape[0]
    n_experts = w_expert.shape[1]
    tm_e = 256 if m >= 4096 else 32
    route, counts = _router(h, norm_ffn, w_group, b_group, w_expert, b_expert)
    pos, tile_expert, n_used, pad_start, pad_len = _route_plan(route, counts, tm_e)
    n_rows = 2 * m + n_experts * tm_e
    xs = _moe_dispatch(h, pos, pad_start, pad_len, n_rows)
    y_sorted = _moe_experts(xs, norm_ffn, tile_expert, n_used, w_gate, w_up, w_down, layer, tm_e)
    tm_c = _row_tile(m, 256)
    return _combine_ple(h, route, pos, y_sorted, p, win_bf, g_ple, wgate_bf, g_final, tm_c, final_norm)


def _rope_block(x, c, s):
    return x * c + pltpu.roll(x, 64, 1) * s


def _mla_proj_kernel(x_ref, gmix_ref, wdq_ref, gq_ref, wuq_ref, wdkv_ref, gkv_ref, cos_ref, sin_ref,
                     qn_ref, qr_ref, ckv_ref, kr_ref, *, scale, kv_lora, n_pairs):
    hn = _rms(x_ref[...], gmix_ref[...]).astype(BF16)
    cq = _rms(_dot(hn, wdq_ref[...]), gq_ref[...]).astype(BF16)
    q = _dot(cq, wuq_ref[...]) * scale
    nn = n_pairs * LANES
    c, s = cos_ref[...], sin_ref[...]
    qn_ref[...] = q[:, :nn]
    for p in range(n_pairs):
        blk = q[:, nn + p * LANES:nn + (p + 1) * LANES]
        qr_ref[:, p * LANES:(p + 1) * LANES] = _rope_block(blk, c, s)
    kv = _dot(hn, wdkv_ref[...])
    ckv_ref[...] = _rms(kv[:, :kv_lora], gkv_ref[...])
    kr_ref[...] = _rope_block(kv[:, kv_lora:], c, s)


def _mla_proj_prompt_kernel(x_ref, gmix_ref, wdq_ref, gq_ref, wqt_ref, wdkv_ref, gkv_ref, cos_ref, sin_ref,
                            cost_ref, sint_ref, wuk_ref, wuvt_ref,
                            qt_ref, ckv_ref, kr_ref, kn_ref, krb_ref, vt_ref, *, scale, kv_lora, n_pairs):
    tm = x_ref.shape[0]
    half = LANES // 2
    hn = _rms(x_ref[...], gmix_ref[...]).astype(BF16)
    cq = _rms(_dot(hn, wdq_ref[...]), gq_ref[...]).astype(BF16)
    q3 = (_dot_nt(wqt_ref[...], cq) * scale).reshape(n_pairs, 2 * LANES, tm)
    rope = q3[:, LANES:, :]
    rolled = jnp.concatenate([rope[:, half:, :], rope[:, :half, :]], axis=1)
    qt_ref[:, :LANES, :] = q3[:, :LANES, :].astype(BF16)
    qt_ref[:, LANES:, :] = (rope * cost_ref[...][None] + rolled * sint_ref[...][None]).astype(BF16)
    kv = _dot(hn, wdkv_ref[...])
    ckv = _rms(kv[:, :kv_lora], gkv_ref[...])
    kr = _rope_block(kv[:, kv_lora:], cos_ref[...], sin_ref[...])
    ckv_ref[...] = ckv
    kr_ref[...] = kr
    cb = ckv.astype(BF16)
    kn_ref[...] = _dot(cb, wuk_ref[...]).astype(BF16)
    krb_ref[...] = kr.astype(BF16)
    vt_ref[...] = _dot_nt(wuvt_ref[...], cb).reshape(n_pairs, LANES, tm).astype(BF16)


def _rope_tables(pos, rope_dim):
    half = rope_dim // 2
    inv_freq = jnp.power(jnp.float32(ROPE_THETA), -jnp.arange(half, dtype=F32) / half)
    ang = pos.astype(F32)[:, None] * inv_freq[None, :]
    cos, sin = jnp.cos(ang), jnp.sin(ang)
    z = jnp.zeros((pos.shape[0], 2 * half), F32)
    c = jnp.concatenate([cos, cos, z, cos, cos, z], axis=1)
    s = jnp.concatenate([-sin, -sin, z, sin, sin, z], axis=1)
    return c, s


def _mla_weights(w_dq, w_uq, w_dkv, w_uk, w_uv, n_heads, nope, rope_dim, kv_lora):
    half = rope_dim // 2
    assert 2 * nope == LANES and 4 * half <= LANES // 2
    n_pairs = n_heads // 2
    hd = nope + rope_dim
    idx_n = np.array([h * hd + n for h in range(n_heads) for n in range(nope)])
    w_qn = w_uq[:, idx_n]
    zq = jnp.zeros((w_uq.shape[0], LANES // 2 - 2 * half), F32)
    blocks = []
    for p in range(n_pairs):
        h0, h1 = 2 * p, 2 * p + 1
        x1 = lambda h: w_uq[:, h * hd + nope:h * hd + nope + half]
        x2 = lambda h: w_uq[:, h * hd + nope + half:h * hd + hd]
        blocks += [x1(h0), x1(h1), zq, x2(h0), x2(h1), zq]
    w_qr = jnp.concatenate(blocks, axis=1)
    w_q = jnp.concatenate([w_qn, w_qr], axis=1).astype(BF16)
    w_qt = jnp.concatenate(
        [jnp.concatenate([w_qn[:, p * LANES:(p + 1) * LANES].T, w_qr[:, p * LANES:(p + 1) * LANES].T], axis=0)
         for p in range(n_pairs)], axis=0).astype(BF16)
    zk = jnp.zeros((w_dkv.shape[0], LANES // 2 - 2 * half), F32)
    k1 = w_dkv[:, kv_lora:kv_lora + half]
    k2 = w_dkv[:, kv_lora + half:]
    w_kv = jnp.concatenate([w_dkv[:, :kv_lora], k1, k1, zk, k2, k2, zk], axis=1).astype(BF16)
    w_k = w_uk.reshape(kv_lora, n_heads * nope).astype(BF16)
    w_vt = w_uv.reshape(kv_lora, -1).T.astype(BF16)
    return w_dq.astype(BF16), w_q, w_kv, w_k, w_qt, w_vt


def _mla_proj(h, g_mix, wts, g_q, g_kv, cos, sin, scale):
    m, d = h.shape
    w_dq, w_q, w_kv, w_k = wts[:4]
    q_lora = w_dq.shape[1]
    kv_lora = w_k.shape[0]
    n_pairs = w_k.shape[1] // LANES
    nn = n_pairs * LANES
    tm = _row_tile(m, 512)
    tab = pl.BlockSpec((1, LANES), lambda i: (0, 0))
    const = lambda a: pl.BlockSpec(a.shape, lambda i: (0,) * a.ndim)
    row = lambda n: pl.BlockSpec((tm, n), lambda i: (i, 0))
    gq2, gkv2, gm2 = g_q.reshape(1, q_lora), g_kv.reshape(1, kv_lora), g_mix.reshape(1, d)
    return pl.pallas_call(
        functools.partial(_mla_proj_kernel, scale=scale, kv_lora=kv_lora, n_pairs=n_pairs),
        grid=(m // tm,),
        in_specs=[row(d), const(gm2), const(w_dq), const(gq2), const(w_q), const(w_kv), const(gkv2),
                  tab, tab],
        out_specs=[row(nn), row(nn), row(kv_lora), row(LANES)],
        out_shape=[jax.ShapeDtypeStruct((m, nn), F32), jax.ShapeDtypeStruct((m, nn), F32),
                   jax.ShapeDtypeStruct((m, kv_lora), F32), jax.ShapeDtypeStruct((m, LANES), F32)],
        compiler_params=_cp("parallel"),
        name="mla_proj",
    )(h, gm2, w_dq, gq2, w_q, w_kv, gkv2, cos, sin)


def _mla_proj_prompt(h, g_mix, wts, g_q, g_kv, cos, sin, scale):
    m, d = h.shape
    w_dq, _, w_kv, w_k, w_qt, w_vt = wts
    q_lora = w_dq.shape[1]
    kv_lora = w_k.shape[0]
    n_pairs = w_k.shape[1] // LANES
    nn = n_pairs * LANES
    t = cos.shape[0]
    tm = _row_tile(t, 512)
    nt = t // tm
    tab = pl.BlockSpec((tm, LANES), lambda i: (i % nt, 0))
    tab_t = pl.BlockSpec((LANES, tm), lambda i: (0, i % nt))
    const = lambda a: pl.BlockSpec(a.shape, lambda i: (0,) * a.ndim)
    row = lambda n: pl.BlockSpec((tm, n), lambda i: (i, 0))
    col = lambda n: pl.BlockSpec((n_pairs, n, tm), lambda i: (0, 0, i))
    gq2, gkv2, gm2 = g_q.reshape(1, q_lora), g_kv.reshape(1, kv_lora), g_mix.reshape(1, d)
    return pl.pallas_call(
        functools.partial(_mla_proj_prompt_kernel, scale=scale, kv_lora=kv_lora, n_pairs=n_pairs),
        grid=(m // tm,),
        in_specs=[row(d), const(gm2), const(w_dq), const(gq2), const(w_qt), const(w_kv), const(gkv2),
                  tab, tab, tab_t, tab_t, const(w_k), const(w_vt)],
        out_specs=[col(2 * LANES), row(kv_lora), row(LANES), row(nn), row(LANES), col(LANES)],
        out_shape=[jax.ShapeDtypeStruct((n_pairs, 2 * LANES, m), BF16),
                   jax.ShapeDtypeStruct((m, kv_lora), F32), jax.ShapeDtypeStruct((m, LANES), F32),
                   jax.ShapeDtypeStruct((m, nn), BF16), jax.ShapeDtypeStruct((m, LANES), BF16),
                   jax.ShapeDtypeStruct((n_pairs, LANES, m), BF16)],
        compiler_params=_cp("parallel"),
        name="mla_proj_prompt",
    )(h, gm2, w_dq, gq2, w_qt, w_kv, gkv2, cos, sin, cos.T, sin.T, w_k, w_vt)


def _attn_prompt_kernel(qt_ref, kn_ref, kr_ref, vt_ref, o_ref, *, tq):
    i = pl.program_id(2)
    half = LANES // 2
    qt = qt_ref[0].astype(F32)
    row = lax.broadcasted_iota(jnp.int32, qt.shape, 0)
    ws = []
    for hh in range(2):
        sel = (((row < LANES) & ((row // half) == hh))
               | ((row >= LANES) & (((row % half) // 16) == hh)))
        ws.append(jnp.where(sel, qt, 0.0).astype(BF16))

    ones = jnp.ones((16, tq), BF16)

    def block(j, carry, masked):
        r0 = pl.multiple_of(j * tq, tq)
        k = jnp.concatenate([kn_ref[0, pl.ds(r0, tq), :], kr_ref[0, pl.ds(r0, tq), :]], axis=1)
        vt = vt_ref[0, :, pl.ds(r0, tq)]
        out = []
        for hh in range(2):
            m, acc = carry[hh]
            st = _dot(k, ws[hh])
            if masked:
                key = lax.broadcasted_iota(jnp.int32, st.shape, 0)
                qry = lax.broadcasted_iota(jnp.int32, st.shape, 1)
                st = jnp.where(key <= qry, st, -jnp.inf)
            m_new = jnp.maximum(m, jnp.max(st, axis=0, keepdims=True))
            alpha = jnp.exp2(m - m_new)
            pt = jnp.exp2(st - m_new).astype(BF16)
            va = jnp.concatenate([vt[hh * half:(hh + 1) * half, :], ones], axis=0)
            out.append((m_new, alpha * acc + _dot(va, pt)))
        return tuple(out)

    init = tuple((jnp.full((1, tq), -jnp.inf, F32), jnp.zeros((half + 16, tq), F32)) for _ in range(2))
    carry = lax.fori_loop(0, i, lambda j, c: block(j, c, False), init)
    (_, a0), (_, a1) = block(i, carry, True)
    ot = jnp.concatenate([a0[:half] / a0[half:half + 1], a1[:half] / a1[half:half + 1]], axis=0)
    o_ref[0] = ot.T.astype(o_ref.dtype)


def _attn_prompt(qt, kn, kr, vt, bsz, t):
    n_pairs = qt.shape[0]
    tq = _row_tile(t, 512)
    nq = t // tq
    r3 = lambda a: a.reshape(bsz, t, a.shape[-1])
    return pl.pallas_call(
        functools.partial(_attn_prompt_kernel, tq=tq),
        grid=(bsz, n_pairs, nq),
        in_specs=[pl.BlockSpec((1, 2 * LANES, tq), lambda b, p, i: (p, 0, b * nq + i)),
                  pl.BlockSpec((1, t, LANES), lambda b, p, i: (b, 0, p)),
                  pl.BlockSpec((1, t, LANES), lambda b, p, i: (b, 0, 0)),
                  pl.BlockSpec((1, LANES, t), lambda b, p, i: (p, 0, b))],
        out_specs=pl.BlockSpec((1, tq, LANES), lambda b, p, i: (b, i, p)),
        out_shape=jax.ShapeDtypeStruct((bsz, t, n_pairs * LANES), BF16),
        compiler_params=_cp("parallel", "parallel", "arbitrary"),
        name="attn_prompt",
    )(qt, r3(kn), r3(kr), vt)


def _out_proj_kernel(h_ref, o_ref, w_ref, y_ref):
    y_ref[...] = h_ref[...] + _dot(o_ref[...], w_ref[...])


def _out_proj(h, o_bf, wo_bf):
    m, d = h.shape
    n = o_bf.shape[1]
    tm = _row_tile(m, 512)
    return pl.pallas_call(
        _out_proj_kernel,
        grid=(m // tm,),
        in_specs=[pl.BlockSpec((tm, d), lambda i: (i, 0)), pl.BlockSpec((tm, n), lambda i: (i, 0)),
                  pl.BlockSpec((n, d), lambda i: (0, 0))],
        out_specs=pl.BlockSpec((tm, d), lambda i: (i, 0)),
        out_shape=jax.ShapeDtypeStruct((m, d), F32),
        compiler_params=_cp("parallel"),
        name="attn_out_proj",
    )(h, o_bf, wo_bf)


def _q_absorb_kernel(qn_ref, wukt_ref, o_ref, *, n_pairs, kv_lora):
    lane = lax.broadcasted_iota(jnp.int32, (1, LANES), 1)
    half = LANES // 2
    for p in range(n_pairs):
        blk = qn_ref[:, p * LANES:(p + 1) * LANES]
        for hh in range(2):
            qm = jnp.where((lane // half) == hh, blk, 0.0).astype(BF16)
            h = 2 * p + hh
            o_ref[:, h * kv_lora:(h + 1) * kv_lora] = _dot(qm, wukt_ref[p])


def _q_absorb(qn, wukt_bf):
    m = qn.shape[0]
    n_pairs, _, kv_lora = wukt_bf.shape
    return pl.pallas_call(
        functools.partial(_q_absorb_kernel, n_pairs=n_pairs, kv_lora=kv_lora),
        out_shape=jax.ShapeDtypeStruct((m, 2 * n_pairs * kv_lora), F32),
        compiler_params=pltpu.CompilerParams(vmem_limit_bytes=VMEM_LIMIT),
        name="q_absorb",
    )(qn, wukt_bf)


def _attn_decode_kernel(pt_ref, qlat_ref, qpe_ref, cnew_ref, knew_ref, ckv_hbm, kpe_hbm, o_ref,
                        cbuf, kbuf, sem, *, n_pages, page, layer, n_chunks):
    b = pl.program_id(0)
    nb = pl.num_programs(0)

    def copies(seq, slot, j):
        pg = pt_ref[seq, j]
        return (pltpu.make_async_copy(ckv_hbm.at[pg, layer], cbuf.at[slot, pl.ds(j * page, page), :],
                                      sem.at[0, slot]),
                pltpu.make_async_copy(kpe_hbm.at[pg, layer], kbuf.at[slot, :, pl.ds(j * page, page)],
                                      sem.at[1, slot]))

    def start_seq(seq, slot):
        def body(j, c):
            for cp in copies(seq, slot, j):
                cp.start()
            return c
        lax.fori_loop(0, n_pages, body, 0, unroll=4)

    def wait_seq(seq, slot):
        def body(j, c):
            for cp in copies(seq, slot, j):
                cp.wait()
            return c
        lax.fori_loop(0, n_pages, body, 0, unroll=4)

    @pl.when(b == 0)
    def _():
        start_seq(0, 0)

    @pl.when(b + 1 < nb)
    def _():
        start_seq(b + 1, (b + 1) % 2)

    slot = b % 2
    wait_seq(b, slot)
    ql = qlat_ref[0].astype(BF16)
    qp = qpe_ref[0].astype(BF16)
    cnew = cnew_ref[0].astype(BF16).astype(F32)
    knew = knew_ref[0].astype(BF16).astype(F32)
    s_new = (jnp.sum(ql.astype(F32) * cnew, axis=-1, keepdims=True)
             + jnp.sum(qp.astype(F32) * knew, axis=-1, keepdims=True))
    ch = (n_pages * page) // n_chunks

    def chunk(state, c):
        m, l, acc = state
        ck = cbuf[slot, c * ch:(c + 1) * ch, :].astype(BF16)
        kt = kbuf[slot, :, c * ch:(c + 1) * ch].astype(BF16)
        s = _dot_nt(ql, ck) + _dot(qp, kt)
        m_new = jnp.maximum(m, jnp.max(s, axis=-1, keepdims=True))
        alpha = jnp.exp(m - m_new)
        p = jnp.exp(s - m_new)
        return (m_new, alpha * l + jnp.sum(p, axis=-1, keepdims=True),
                alpha * acc + _dot(p.astype(BF16), ck))

    n_a = (n_chunks + 1) // 2
    st_a = (s_new, jnp.ones_like(s_new), jnp.broadcast_to(cnew, (ql.shape[0], cnew.shape[1])))
    for c in range(n_a):
        st_a = chunk(st_a, c)
    if n_chunks > n_a:
        st_b = (jnp.full_like(s_new, -jnp.inf), jnp.zeros_like(s_new), jnp.zeros_like(st_a[2]))
        for c in range(n_a, n_chunks):
            st_b = chunk(st_b, c)
        m = jnp.maximum(st_a[0], st_b[0])
        fa, fb = jnp.exp(st_a[0] - m), jnp.exp(st_b[0] - m)
        l = fa * st_a[1] + fb * st_b[1]
        acc = fa * st_a[2] + fb * st_b[2]
    else:
        _, l, acc = st_a
    o_ref[0] = acc / l


def _attn_decode(page_table, qlat, qpe, cnew, knew, cache_ckv, cache_kpe_t, layer):
    bsz, n_heads, kv_lora = qlat.shape
    rope_dim = qpe.shape[-1]
    n_pages = page_table.shape[1]
    page = cache_ckv.shape[2]
    n_chunks = 4 if (n_pages * page) % (4 * LANES) == 0 else 1
    per_seq = lambda n, w: pl.BlockSpec((1, n, w), lambda b, pt: (b, 0, 0))
    grid_spec = pltpu.PrefetchScalarGridSpec(
        num_scalar_prefetch=1,
        grid=(bsz,),
        in_specs=[per_seq(n_heads, kv_lora), per_seq(n_heads, rope_dim), per_seq(1, kv_lora),
                  per_seq(1, rope_dim), pl.BlockSpec(memory_space=pl.ANY),
                  pl.BlockSpec(memory_space=pl.ANY)],
        out_specs=per_seq(n_heads, kv_lora),
        scratch_shapes=[pltpu.VMEM((2, n_pages * page, kv_lora), F32),
                        pltpu.VMEM((2, rope_dim, n_pages * page), F32),
                        pltpu.SemaphoreType.DMA((2, 2))],
    )
    return pl.pallas_call(
        functools.partial(_attn_decode_kernel, n_pages=n_pages, page=page, layer=layer,
                          n_chunks=n_chunks),
        grid_spec=grid_spec,
        out_shape=jax.ShapeDtypeStruct((bsz, n_heads, kv_lora), F32),
        compiler_params=_cp("arbitrary"),
        name="attn_decode",
    )(page_table, qlat, qpe, cnew, knew, cache_ckv, cache_kpe_t)


def _decode_out_kernel(h_ref, olat_ref, wuv_ref, wo_ref, y_ref, o_scr, *, n_pairs, kv_lora):
    for p in range(n_pairs):
        blk = olat_ref[:, 2 * p * kv_lora:(2 * p + 2) * kv_lora].astype(BF16)
        o_scr[:, p * LANES:(p + 1) * LANES] = _dot(blk, wuv_ref[p]).astype(BF16)
    y_ref[...] = h_ref[...] + _dot(o_scr[...], wo_ref[...])


def _decode_out(h, olat, wuv_bd_bf, wo_bf):
    m, d = h.shape
    n_pairs = wuv_bd_bf.shape[0]
    kv_lora = wuv_bd_bf.shape[1] // 2
    return pl.pallas_call(
        functools.partial(_decode_out_kernel, n_pairs=n_pairs, kv_lora=kv_lora),
        out_shape=jax.ShapeDtypeStruct((m, d), F32),
        scratch_shapes=[pltpu.VMEM((m, n_pairs * LANES), BF16)],
        compiler_params=pltpu.CompilerParams(vmem_limit_bytes=VMEM_LIMIT),
        name="decode_out_proj",
    )(h, olat, wuv_bd_bf, wo_bf)


def kernel(x_prompt, x_sample, p_prompt, p_sample, state_conv, cache_ckv, cache_kpe, page_table, norm_mix, norm_ffn, norm_final, conv_w_pw1, conv_b_pw1, conv_w_dw, conv_b_dw, conv_ln_g, conv_ln_b, conv_w_pw2, mla_w_dq, mla_g_q, mla_w_uq, mla_w_dkv, mla_g_kv, mla_w_uk, mla_w_uv, mla_w_o, moe_w_group, moe_b_group, moe_w_expert, moe_b_expert, moe_w_gate, moe_w_up, moe_w_down, ple_w_in, ple_g, ple_w_gate):
    bsz, t, d = x_prompt.shape
    dbsz, dt, _ = x_sample.shape
    depth = p_prompt.shape[0]
    assert depth == 2 and dt == 1
    kv_lora, n_heads, nope = mla_w_uk.shape[1:]
    rope_dim = mla_w_dkv.shape[2] - kv_lora
    vdim = mla_w_uv.shape[3]
    past_len = page_table.shape[1] * cache_ckv.shape[2]
    scale = 1.0 / math.sqrt(nope + rope_dim)
    width = conv_w_dw.shape[1]
    bf = lambda a: a.astype(BF16)

    mp, ms = bsz * t, dbsz * dt
    hp = x_prompt.reshape(mp, d)
    hs = x_sample.reshape(ms, d)

    w1, w2 = bf(conv_w_pw1[0]), bf(conv_w_pw2[0])
    conv_args = (conv_w_dw[0], conv_b_dw[0], conv_ln_g[0], conv_ln_b[0], w2)
    glu_p = _pw1_glu(hp, norm_mix[0], w1, conv_b_pw1[0]).reshape(bsz, t, d)
    hp = _conv_prompt(glu_p, hp.reshape(bsz, t, d), *conv_args).reshape(mp, d)
    conv_state_prompt = glu_p[:, t - (width - 1):][None]
    glu_s = _pw1_glu(hs, norm_mix[0], w1, conv_b_pw1[0])
    buf = state_conv[0]
    hs = _conv_sample(jnp.swapaxes(buf, 0, 1), glu_s, hs, *conv_args)
    conv_state_sample = jnp.concatenate([buf[:, 1:], glu_s[:, None, :]], axis=1)[None]

    def moe_ple(h, p, i, final_norm):
        return _moe_ple_layer(h, p, norm_ffn[i], moe_w_group[i], moe_b_group[i], moe_w_expert[i],
                              moe_b_expert[i], moe_w_gate, moe_w_up, moe_w_down, i,
                              bf(ple_w_in[i]), ple_g[i], bf(ple_w_gate[i]), norm_final, final_norm)

    hp = moe_ple(hp, p_prompt[0].reshape(mp, -1), 0, False)
    hs = moe_ple(hs, p_sample[0].reshape(ms, -1), 0, False)

    wts = _mla_weights(mla_w_dq[0], mla_w_uq[0], mla_w_dkv[0], mla_w_uk[0], mla_w_uv[0],
                       n_heads, nope, rope_dim, kv_lora)
    wo = bf(mla_w_o[0])
    half = rope_dim // 2
    natural = lambda kr: jnp.concatenate([kr[:, :half], kr[:, LANES // 2:LANES // 2 + half]], axis=1)

    cos_p, sin_p = _rope_tables(jnp.arange(t, dtype=jnp.int32), rope_dim)
    qt, ckv_p, kr_p, kn, krb, vt = _mla_proj_prompt(hp, norm_mix[1], wts, mla_g_q[0], mla_g_kv[0],
                                                    cos_p, sin_p, scale * math.log2(math.e))
    o_p = _attn_prompt(qt, kn, krb, vt, bsz, t)
    hp = _out_proj(hp, o_p.reshape(mp, -1), wo)
    ckv_prompt = ckv_p.reshape(1, bsz, t, kv_lora)
    kpe_prompt = natural(kr_p).reshape(1, bsz, t, rope_dim)

    cos_s, sin_s = _rope_tables(past_len + jnp.arange(dt, dtype=jnp.int32), rope_dim)
    qn_s, qr_s, ckv_s, kr_s = _mla_proj(hs, norm_mix[1], wts, mla_g_q[0], mla_g_kv[0],
                                        cos_s, sin_s, scale)
    n_pairs = n_heads // 2
    wukt = bf(jnp.transpose(mla_w_uk[0], (1, 2, 0)).reshape(n_pairs, 2 * nope, kv_lora))
    qlat = _q_absorb(qn_s, wukt).reshape(ms, n_heads, kv_lora)
    qr3 = qr_s.reshape(ms, n_pairs, LANES)
    x1 = qr3[:, :, :2 * half].reshape(ms, n_heads, half)
    x2 = qr3[:, :, LANES // 2:LANES // 2 + 2 * half].reshape(ms, n_heads, half)
    qpe = jnp.concatenate([x1, x2], axis=-1)
    kpe_s = natural(kr_s)
    olat = _attn_decode(page_table, qlat, qpe, ckv_s.reshape(ms, 1, kv_lora),
                        kpe_s.reshape(ms, 1, rope_dim), cache_ckv, jnp.swapaxes(cache_kpe, 2, 3), 0)
    wuv = mla_w_uv[0]
    zv = jnp.zeros((kv_lora, vdim), F32)
    wuv_bd = jnp.stack([jnp.concatenate([jnp.concatenate([wuv[:, 2 * p], zv], axis=1),
                                         jnp.concatenate([zv, wuv[:, 2 * p + 1]], axis=1)], axis=0)
                        for p in range(n_pairs)])
    hs = _decode_out(hs, olat.reshape(ms, n_heads * kv_lora), bf(wuv_bd), wo)
    ckv_sample = ckv_s.reshape(1, dbsz, dt, kv_lora)
    kpe_sample = kpe_s.reshape(1, dbsz, dt, rope_dim)

    y_prompt = moe_ple(hp, p_prompt[1].reshape(mp, -1), 1, True).reshape(bsz, t, d)
    y_sample = moe_ple(hs, p_sample[1].reshape(ms, -1), 1, True).reshape(dbsz, dt, d)
    return (y_prompt, y_sample, conv_state_prompt, conv_state_sample,
            ckv_prompt, kpe_prompt, ckv_sample, kpe_sample)
```

```python
import functools
import math

import jax
import jax.numpy as jnp
import numpy as np
from jax import lax
from jax.experimental import pallas as pl
from jax.experimental.pallas import tpu as pltpu

F32 = jnp.float32
BF16 = jnp.bfloat16

EPS = 1e-6
ROPE_THETA = 10000.0
CONV_HALO = 32
LANES = 128
VMEM_LIMIT = 56 * 1024 * 1024


def _cp(*sem):
    return pltpu.CompilerParams(dimension_semantics=sem, vmem_limit_bytes=VMEM_LIMIT)


def _rms(x, g):
    return x * lax.rsqrt(jnp.mean(x * x, axis=-1, keepdims=True) + EPS) * g


def _dot(a, b):
    return jnp.dot(a, b, preferred_element_type=F32)


def _dot_nt(a, b):
    return lax.dot_general(a, b, (((1,), (1,)), ((), ())), preferred_element_type=F32)


def _row_tile(m, target):
    t = min(m, target)
    while m % t:
        t //= 2
    return t


def _pw1_glu_kernel(x_ref, g_ref, w_ref, b_ref, o_ref):
    d = o_ref.shape[-1]
    hn = _rms(x_ref[...], g_ref[...]).astype(BF16)
    a = _dot(hn, w_ref[:, :d]) + b_ref[:, :d]
    b = _dot(hn, w_ref[:, d:]) + b_ref[:, d:]
    o_ref[...] = a * jax.nn.sigmoid(b)


def _pw1_glu(x, g, w_bf, b):
    m, d = x.shape
    tm = _row_tile(m, 512)
    return pl.pallas_call(
        _pw1_glu_kernel,
        grid=(m // tm,),
        in_specs=[
            pl.BlockSpec((tm, d), lambda i: (i, 0)),
            pl.BlockSpec((1, d), lambda i: (0, 0)),
            pl.BlockSpec((d, 2 * d), lambda i: (0, 0)),
            pl.BlockSpec((1, 2 * d), lambda i: (0, 0)),
        ],
        out_specs=pl.BlockSpec((tm, d), lambda i: (i, 0)),
        out_shape=jax.ShapeDtypeStruct((m, d), F32),
        compiler_params=_cp("parallel"),
        name="pw1_glu",
    )(x, g.reshape(1, d), w_bf, b.reshape(1, 2 * d))


def _ln_silu(z, g, b):
    mu = jnp.mean(z, axis=-1, keepdims=True)
    zc = z - mu
    var = jnp.mean(zc * zc, axis=-1, keepdims=True)
    y = zc * lax.rsqrt(var + EPS) * g + b
    return y * jax.nn.sigmoid(y)


def _conv_prompt_kernel(cur_ref, halo_ref, h_ref, wdw_ref, bdw_ref, lng_ref, lnb_ref, w2_ref,
                        o_ref, full_ref, sh_ref, z_ref, *, width, rows, lanes):
    tq, d = cur_ref.shape[1], cur_ref.shape[2]
    i = pl.program_id(1)
    full_ref[0:CONV_HALO, :] = jnp.where(i > 0, halo_ref[0], 0.0)
    full_ref[CONV_HALO:CONV_HALO + tq, :] = cur_ref[0]
    off = CONV_HALO - (width - 1)
    sub = 8
    for l0 in range(0, d, lanes):
        taps = [len(range(r, width, sub)) for r in range(sub)]
        for r in range(sub):
            n = tq + sub * (taps[r] - 1)
            sh_ref[r, 0:n, :] = full_ref[off + r:off + r + n, l0:l0 + lanes]
        for r0 in range(0, tq, rows):
            acc = jnp.zeros((rows, lanes), F32)
            for k in range(width):
                a, r = divmod(k, sub)
                acc = acc + sh_ref[r, r0 + sub * a:r0 + sub * a + rows, :] * wdw_ref[k:k + 1, l0:l0 + lanes]
            z_ref[r0:r0 + rows, l0:l0 + lanes] = acc
    y = _ln_silu(z_ref[...] + bdw_ref[...], lng_ref[...], lnb_ref[...])
    o_ref[0] = h_ref[0] + _dot(y.astype(BF16), w2_ref[...])


def _conv_prompt(glu, h, w_dw, b_dw, ln_g, ln_b, w2_bf):
    bsz, t, d = glu.shape
    width = w_dw.shape[0]
    tq = _row_tile(t, 512)
    hb = tq // CONV_HALO
    wpad = jnp.zeros((CONV_HALO, d), F32).at[:width].set(w_dw)
    lanes = 256
    kern = functools.partial(_conv_prompt_kernel, width=width, rows=min(64, tq), lanes=lanes)
    vec = lambda a: a.reshape(1, d)
    cvec = pl.BlockSpec((1, d), lambda b, i: (0, 0))
    return pl.pallas_call(
        kern,
        grid=(bsz, t // tq),
        in_specs=[
            pl.BlockSpec((1, tq, d), lambda b, i: (b, i, 0)),
            pl.BlockSpec((1, CONV_HALO, d), lambda b, i: (b, jnp.maximum(i * hb - 1, 0), 0)),
            pl.BlockSpec((1, tq, d), lambda b, i: (b, i, 0)),
            pl.BlockSpec((CONV_HALO, d), lambda b, i: (0, 0)),
            cvec, cvec, cvec,
            pl.BlockSpec((d, d), lambda b, i: (0, 0)),
        ],
        out_specs=pl.BlockSpec((1, tq, d), lambda b, i: (b, i, 0)),
        out_shape=jax.ShapeDtypeStruct((bsz, t, d), F32),
        scratch_shapes=[pltpu.VMEM((tq + CONV_HALO, d), F32),
                        pltpu.VMEM((8, tq + CONV_HALO, lanes), F32), pltpu.VMEM((tq, d), F32)],
        compiler_params=_cp("parallel", "arbitrary"),
        name="conv_prompt",
    )(glu, glu, h, wpad, vec(b_dw), vec(ln_g), vec(ln_b), w2_bf)


def _conv_sample_kernel(buf_ref, u_ref, h_ref, wdw_ref, bdw_ref, lng_ref, lnb_ref, w2_ref, o_ref,
                        *, width):
    acc = u_ref[...] * wdw_ref[width - 1:width, :]
    for k in range(width - 1):
        acc = acc + buf_ref[k] * wdw_ref[k:k + 1, :]
    y = _ln_silu(acc + bdw_ref[...], lng_ref[...], lnb_ref[...])
    o_ref[...] = h_ref[...] + _dot(y.astype(BF16), w2_ref[...])


def _conv_sample(buf_t, u, h, w_dw, b_dw, ln_g, ln_b, w2_bf):
    nb, bsz, d = buf_t.shape
    width = w_dw.shape[0]
    bb = _row_tile(bsz, 32)
    wpad = jnp.zeros((CONV_HALO, d), F32).at[:width].set(w_dw)
    vec = lambda a: a.reshape(1, d)
    cvec = pl.BlockSpec((1, d), lambda i: (0, 0))
    row = pl.BlockSpec((bb, d), lambda i: (i, 0))
    return pl.pallas_call(
        functools.partial(_conv_sample_kernel, width=width),
        grid=(bsz // bb,),
        in_specs=[
            pl.BlockSpec((nb, bb, d), lambda i: (0, i, 0)),
            row, row,
            pl.BlockSpec((CONV_HALO, d), lambda i: (0, 0)),
            cvec, cvec, cvec,
            pl.BlockSpec((d, d), lambda i: (0, 0)),
        ],
        out_specs=row,
        out_shape=jax.ShapeDtypeStruct((bsz, d), F32),
        compiler_params=_cp("parallel"),
        name="conv_sample",
    )(buf_t, u, h, wpad, vec(b_dw), vec(ln_g), vec(ln_b), w2_bf)


def _router_kernel(x_ref, g_ref, whi_ref, wlo_ref, b_ref, o_ref, cnt_ref, *, n_groups, per_group):
    hn = _rms(x_ref[...], g_ref[...])
    hn_hi = hn.astype(BF16)
    hn_lo = (hn - hn_hi.astype(F32)).astype(BF16)
    logits = (_dot(hn_hi, whi_ref[...]) + _dot(hn_lo, whi_ref[...]) + _dot(hn_hi, wlo_ref[...])
              + b_ref[...])
    lane = lax.broadcasted_iota(jnp.int32, logits.shape, 1)
    neg = jnp.float32(-jnp.inf)
    big = jnp.int32(1 << 20)

    def first_argmax(v):
        m = jnp.max(v, axis=-1, keepdims=True)
        return m, jnp.min(jnp.where(v == m, lane, big), axis=-1, keepdims=True)

    gl = jnp.where(lane < n_groups, logits, neg)
    gmax, grp = first_argmax(gl)
    g_w = 1.0 / jnp.sum(jnp.exp(gl - gmax), axis=-1, keepdims=True)
    lo = n_groups + grp * per_group
    el = jnp.where((lane >= lo) & (lane < lo + per_group), logits, neg)
    m1, i1 = first_argmax(el)
    el2 = jnp.where(lane == i1, neg, el)
    m2, i2 = first_argmax(el2)
    e2 = jnp.exp(m2 - m1)
    w1 = g_w / (1.0 + e2)
    w2 = g_w * e2 / (1.0 + e2)
    id1 = (i1 - n_groups).astype(F32)
    id2 = (i2 - n_groups).astype(F32)

    @pl.when(pl.program_id(0) == 0)
    def _():
        cnt_ref[...] = jnp.zeros_like(cnt_ref)

    tm = logits.shape[0]
    picks = ((lane == i1) | (lane == i2)).astype(BF16)
    r_io = lax.broadcasted_iota(jnp.int32, (tm, tm), 0)
    c_io = lax.broadcasted_iota(jnp.int32, (tm, tm), 1)
    before = _dot((c_io < r_io).astype(BF16), picks) + cnt_ref[...]
    rank1 = jnp.sum(jnp.where(lane == i1, before, 0.0), axis=-1, keepdims=True)
    rank2 = jnp.sum(jnp.where(lane == i2, before, 0.0), axis=-1, keepdims=True)
    cnt_ref[...] = cnt_ref[...] + jnp.sum(picks.astype(F32), axis=0, keepdims=True)
    vals = (id1, id2, w1, w2, rank1, rank2)
    out = jnp.zeros_like(logits)
    for k, v in enumerate(vals):
        out = jnp.where(lane == k, v, out)
    o_ref[...] = out


def _router(h, g, w_group, b_group, w_expert, b_expert):
    m, d = h.shape
    n_groups, n_experts = w_group.shape[1], w_expert.shape[1]
    tm = _row_tile(m, 512)
    w = jnp.zeros((d, LANES), F32).at[:, :n_groups].set(w_group)
    w = w.at[:, n_groups:n_groups + n_experts].set(w_expert)
    b = jnp.zeros((1, LANES), F32).at[0, :n_groups].set(b_group)
    b = b.at[0, n_groups:n_groups + n_experts].set(b_expert)
    kern = functools.partial(_router_kernel, n_groups=n_groups, per_group=n_experts // n_groups)
    w_hi = w.astype(BF16)
    w_lo = (w - w_hi.astype(F32)).astype(BF16)
    route, counts = pl.pallas_call(
        kern,
        grid=(m // tm,),
        in_specs=[
            pl.BlockSpec((tm, d), lambda i: (i, 0)),
            pl.BlockSpec((1, d), lambda i: (0, 0)),
            pl.BlockSpec((d, LANES), lambda i: (0, 0)),
            pl.BlockSpec((d, LANES), lambda i: (0, 0)),
            pl.BlockSpec((1, LANES), lambda i: (0, 0)),
        ],
        out_specs=[pl.BlockSpec((tm, LANES), lambda i: (i, 0)), pl.BlockSpec((1, LANES), lambda i: (0, 0))],
        out_shape=[jax.ShapeDtypeStruct((m, LANES), F32), jax.ShapeDtypeStruct((1, LANES), F32)],
        compiler_params=_cp("arbitrary"),
        name="moe_router",
    )(h, g.reshape(1, d), w_hi, w_lo, b)
    return route, counts[0, n_groups:n_groups + n_experts].astype(jnp.int32)


def _route_plan(route, counts, tm):
    m = route.shape[0]
    n_experts = counts.shape[0]
    experts = jnp.arange(n_experts, dtype=jnp.int32)
    tiles_per = (counts + tm - 1) // tm
    tile_end = jnp.cumsum(tiles_per)
    start = (tile_end - tiles_per) * tm
    eid = route[:, :2].astype(jnp.int32).T
    rank = route[:, 4:6].astype(jnp.int32).T
    pos = rank + jnp.sum(jnp.where(eid[..., None] == experts, start, 0), axis=-1)
    n_tiles = (2 * m) // tm + n_experts
    tile_ids = jnp.arange(n_tiles, dtype=jnp.int32)
    tile_expert = jnp.minimum(jnp.sum((tile_end[None, :] <= tile_ids[:, None]).astype(jnp.int32), axis=1),
                              n_experts - 1)
    n_used = tile_end[-1:].astype(jnp.int32)
    pad_start = (start + counts).astype(jnp.int32)
    pad_len = (tiles_per * tm - counts).astype(jnp.int32)
    return pos.reshape(-1).astype(jnp.int32), tile_expert.astype(jnp.int32), n_used, pad_start, pad_len


def _moe_dispatch_kernel(pos_ref, ps_ref, pn_ref, nu_ref, h_ref, xs_hbm, zbuf, sem, *, td, m, n_experts, tm,
                         n_tiles):
    i = pl.program_id(0)

    def row_copy(k, r):
        return pltpu.make_async_copy(h_ref.at[pl.ds(r, 1), :],
                                     xs_hbm.at[pl.ds(pos_ref[k * m + i * td + r], 1), :], sem.at[0])

    def pad_copy(e, j):
        return pltpu.make_async_copy(zbuf.at[pl.ds(0, 1), :], xs_hbm.at[pl.ds(ps_ref[e] + j, 1), :], sem.at[1])

    def tail_copy(t):
        return pltpu.make_async_copy(zbuf, xs_hbm.at[pl.ds(pl.multiple_of(t * tm, tm), tm), :], sem.at[1])

    def for_tail(fn):
        lax.fori_loop(nu_ref[0], n_tiles, lambda t, c: (fn(tail_copy(t)), c)[1], 0)

    def for_rows(fn):
        def body(r, c):
            fn(row_copy(0, r), 0)
            fn(row_copy(1, r), 1)
            return c
        lax.fori_loop(0, td, body, 0, unroll=8)

    def for_pads(fn):
        def per_expert(e, c):
            lax.fori_loop(0, pn_ref[e], lambda j, c2: (fn(pad_copy(e, j)), c2)[1], 0)
            return c
        lax.fori_loop(0, n_experts, per_expert, 0)

    for_rows(lambda cp, k: cp.start(priority=k))

    @pl.when(i == 0)
    def _():
        zbuf[...] = jnp.zeros_like(zbuf)
        for_pads(lambda cp: cp.start())
        for_tail(lambda cp: cp.start())
        for_pads(lambda cp: cp.wait())
        for_tail(lambda cp: cp.wait())

    for_rows(lambda cp, k: cp.wait())


def _moe_dispatch(h, pos, pad_start, pad_len, n_used, n_rows, tm):
    m, d = h.shape
    td = _row_tile(m, 512)
    n_experts = pad_start.shape[0]
    grid_spec = pltpu.PrefetchScalarGridSpec(
        num_scalar_prefetch=4,
        grid=(m // td,),
        in_specs=[pl.BlockSpec((td, d), lambda i, pos, ps, pn, nu: (i, 0))],
        out_specs=pl.BlockSpec(memory_space=pl.ANY),
        scratch_shapes=[pltpu.VMEM((tm, d), F32), pltpu.SemaphoreType.DMA((2,))],
    )
    return pl.pallas_call(
        functools.partial(_moe_dispatch_kernel, td=td, m=m, n_experts=n_experts, tm=tm,
                          n_tiles=n_rows // tm),
        grid_spec=grid_spec,
        out_shape=jax.ShapeDtypeStruct((n_rows, d), F32),
        compiler_params=_cp("arbitrary"),
        name="moe_dispatch",
    )(pos, pad_start, pad_len, n_used, h)


def _moe_expert_kernel(te_ref, nu_ref, x_ref, g_ref, wg_ref, wu_ref, wd_ref, o_ref, wgub, wdb):
    i = pl.program_id(0)
    f = wdb.shape[0]

    @pl.when((i == 0) | (te_ref[i] != te_ref[jnp.maximum(i - 1, 0)]))
    def _():
        wgub[:, :f] = wg_ref[0, 0].astype(BF16)
        wgub[:, f:] = wu_ref[0, 0].astype(BF16)
        wdb[...] = wd_ref[0, 0].astype(BF16)

    @pl.when(i < nu_ref[0])
    def _():
        x = _rms(x_ref[...], g_ref[...]).astype(BF16)
        au = _dot(x, wgub[...])
        a, u = au[:, :f], au[:, f:]
        act = (a * jax.nn.sigmoid(a) * u).astype(BF16)
        o_ref[...] = _dot(act, wdb[...])

    @pl.when(i >= nu_ref[0])
    def _():
        o_ref[...] = jnp.zeros_like(o_ref)


def _moe_experts(xs, g, tile_expert, n_used, w_gate, w_up, w_down, layer, tm):
    n_rows, d = xs.shape
    n_tiles = n_rows // tm
    f = w_gate.shape[3]
    xmap = lambda i, te, nu: (jnp.minimum(i, nu[0] - 1), 0)
    grid_spec = pltpu.PrefetchScalarGridSpec(
        num_scalar_prefetch=2,
        grid=(n_tiles,),
        in_specs=[
            pl.BlockSpec((tm, d), xmap),
            pl.BlockSpec((1, d), lambda i, te, nu: (0, 0)),
            pl.BlockSpec((1, 1, d, f), lambda i, te, nu: (layer, te[i], 0, 0)),
            pl.BlockSpec((1, 1, d, f), lambda i, te, nu: (layer, te[i], 0, 0)),
            pl.BlockSpec((1, 1, f, d), lambda i, te, nu: (layer, te[i], 0, 0)),
        ],
        out_specs=pl.BlockSpec((tm, d), lambda i, te, nu: (i, 0)),
        scratch_shapes=[pltpu.VMEM((d, 2 * f), BF16), pltpu.VMEM((f, d), BF16)],
    )
    return pl.pallas_call(
        _moe_expert_kernel,
        grid_spec=grid_spec,
        out_shape=jax.ShapeDtypeStruct((n_rows, d), F32),
        compiler_params=_cp("arbitrary"),
        name="moe_experts",
    )(tile_expert, n_used, xs, g.reshape(1, d), w_gate, w_up, w_down)


def _combine_ple_kernel(pos_ref, h_ref, route_ref, y_hbm, p_ref, win_ref, g_ref, wgate_ref,
                        gfin_ref, o_ref, ybuf, sem, *, tm, final_norm):
    i = pl.program_id(0)
    n = pl.num_programs(0)
    m = n * tm

    def row_copy(tile, slot, k, r):
        return pltpu.make_async_copy(y_hbm.at[pl.ds(pos_ref[k * m + tile * tm + r], 1), :],
                                     ybuf.at[slot, pl.ds(k * tm + r, 1), :], sem.at[slot])

    def start_tile(tile, slot):
        def body(r, c):
            row_copy(tile, slot, 0, r).start(priority=0)
            row_copy(tile, slot, 1, r).start(priority=1)
            return c
        lax.fori_loop(0, tm, body, 0, unroll=8)

    def wait_tile(tile, slot):
        def body(r, c):
            row_copy(tile, slot, 0, r).wait()
            row_copy(tile, slot, 1, r).wait()
            return c
        lax.fori_loop(0, tm, body, 0, unroll=8)

    @pl.when(i == 0)
    def _():
        start_tile(0, 0)

    @pl.when(i + 1 < n)
    def _():
        start_tile(i + 1, (i + 1) % 2)

    slot = i % 2
    wait_tile(i, slot)
    route = route_ref[...]
    h2 = h_ref[...] + route[:, 2:3] * ybuf[slot, 0:tm, :] + route[:, 3:4] * ybuf[slot, tm:2 * tm, :]
    gate = jax.nn.sigmoid(_dot(_rms(h2, g_ref[...]).astype(BF16), wgate_ref[...]))
    h3 = h2 + _dot(p_ref[...].astype(BF16), win_ref[...]) * gate
    if final_norm:
        h3 = _rms(h3, gfin_ref[...])
    o_ref[...] = h3


def _combine_ple(h, route, pos, y_sorted, p, win_bf, g_ple, wgate_bf, g_final, tm, final_norm):
    m, d = h.shape
    pd = p.shape[1]
    grid_spec = pltpu.PrefetchScalarGridSpec(
        num_scalar_prefetch=1,
        grid=(m // tm,),
        in_specs=[
            pl.BlockSpec((tm, d), lambda i, pos: (i, 0)),
            pl.BlockSpec((tm, LANES), lambda i, pos: (i, 0)),
            pl.BlockSpec(memory_space=pl.ANY),
            pl.BlockSpec((tm, pd), lambda i, pos: (i, 0)),
            pl.BlockSpec((pd, d), lambda i, pos: (0, 0)),
            pl.BlockSpec((1, d), lambda i, pos: (0, 0)),
            pl.BlockSpec((d, d), lambda i, pos: (0, 0)),
            pl.BlockSpec((1, d), lambda i, pos: (0, 0)),
        ],
        out_specs=pl.BlockSpec((tm, d), lambda i, pos: (i, 0)),
        scratch_shapes=[pltpu.VMEM((2, 2 * tm, d), F32), pltpu.SemaphoreType.DMA((2,))],
    )
    return pl.pallas_call(
        functools.partial(_combine_ple_kernel, tm=tm, final_norm=final_norm),
        grid_spec=grid_spec,
        out_shape=jax.ShapeDtypeStruct((m, d), F32),
        compiler_params=_cp("arbitrary"),
        name="moe_combine_ple",
    )(pos, h, route, y_sorted, p, win_bf, g_ple.reshape(1, d), wgate_bf, g_final.reshape(1, d))


def _moe_ple_layer(h, p, norm_ffn, w_group, b_group, w_expert, b_expert, w_gate, w_up, w_down, layer,
                   win_bf, g_ple, wgate_bf, g_final, final_norm):
    m = h.shape[0]
    n_experts = w_expert.shape[1]
    tm_e = 256 if m >= 4096 else 32
    route, counts = _router(h, norm_ffn, w_group, b_group, w_expert, b_expert)
    pos, tile_expert, n_used, pad_start, pad_len = _route_plan(route, counts, tm_e)
    n_rows = 2 * m + n_experts * tm_e
    xs = _moe_dispatch(h, pos, pad_start, pad_len, n_used, n_rows, tm_e)
    y_sorted = _moe_experts(xs, norm_ffn, tile_expert, n_used, w_gate, w_up, w_down, layer, tm_e)
    tm_c = _row_tile(m, 256)
    return _combine_ple(h, route, pos, y_sorted, p, win_bf, g_ple, wgate_bf, g_final, tm_c, final_norm)


def _rope_block(x, c, s):
    return x * c + pltpu.roll(x, 64, 1) * s


def _mla_proj_kernel(x_ref, gmix_ref, wdq_ref, gq_ref, wuq_ref, wdkv_ref, gkv_ref, cos_ref, sin_ref,
                     qn_ref, qr_ref, ckv_ref, kr_ref, *, scale, kv_lora, n_pairs):
    hn = _rms(x_ref[...], gmix_ref[...]).astype(BF16)
    cq = _rms(_dot(hn, wdq_ref[...]), gq_ref[...]).astype(BF16)
    q = _dot(cq, wuq_ref[...]) * scale
    nn = n_pairs * LANES
    c, s = cos_ref[...], sin_ref[...]
    qn_ref[...] = q[:, :nn]
    for p in range(n_pairs):
        blk = q[:, nn + p * LANES:nn + (p + 1) * LANES]
        qr_ref[:, p * LANES:(p + 1) * LANES] = _rope_block(blk, c, s)
    kv = _dot(hn, wdkv_ref[...])
    ckv_ref[...] = _rms(kv[:, :kv_lora], gkv_ref[...])
    kr_ref[...] = _rope_block(kv[:, kv_lora:], c, s)


def _mla_proj_prompt_kernel(x_ref, gmix_ref, wdq_ref, gq_ref, wqt_ref, wdkv_ref, gkv_ref, cos_ref, sin_ref,
                            cost_ref, sint_ref, wuk_ref, wuvt_ref,
                            qt_ref, ckv_ref, kr_ref, kn_ref, krb_ref, vt_ref, *, scale, kv_lora, n_pairs):
    tm = x_ref.shape[0]
    half = LANES // 2
    hn = _rms(x_ref[...], gmix_ref[...]).astype(BF16)
    cq = _rms(_dot(hn, wdq_ref[...]), gq_ref[...]).astype(BF16)
    q3 = (_dot_nt(wqt_ref[...], cq) * scale).reshape(n_pairs, 2 * LANES, tm)
    rope = q3[:, LANES:, :]
    rolled = jnp.concatenate([rope[:, half:, :], rope[:, :half, :]], axis=1)
    qt_ref[:, :LANES, :] = q3[:, :LANES, :].astype(BF16)
    qt_ref[:, LANES:, :] = (rope * cost_ref[...][None] + rolled * sint_ref[...][None]).astype(BF16)
    kv = _dot(hn, wdkv_ref[...])
    ckv = _rms(kv[:, :kv_lora], gkv_ref[...])
    kr = _rope_block(kv[:, kv_lora:], cos_ref[...], sin_ref[...])
    ckv_ref[...] = ckv
    kr_ref[...] = kr
    cb = ckv.astype(BF16)
    kn_ref[...] = _dot(cb, wuk_ref[...]).astype(BF16)
    krb_ref[...] = kr.astype(BF16)
    vt_ref[...] = _dot_nt(wuvt_ref[...], cb).reshape(n_pairs, LANES, tm).astype(BF16)


def _rope_tables(pos, rope_dim):
    half = rope_dim // 2
    inv_freq = jnp.power(jnp.float32(ROPE_THETA), -jnp.arange(half, dtype=F32) / half)
    ang = pos.astype(F32)[:, None] * inv_freq[None, :]
    cos, sin = jnp.cos(ang), jnp.sin(ang)
    z = jnp.zeros((pos.shape[0], 2 * half), F32)
    c = jnp.concatenate([cos, cos, z, cos, cos, z], axis=1)
    s = jnp.concatenate([-sin, -sin, z, sin, sin, z], axis=1)
    return c, s


def _mla_weights(w_dq, w_uq, w_dkv, w_uk, w_uv, n_heads, nope, rope_dim, kv_lora):
    half = rope_dim // 2
    assert 2 * nope == LANES and 4 * half <= LANES // 2
    n_pairs = n_heads // 2
    hd = nope + rope_dim
    idx_n = np.array([h * hd + n for h in range(n_heads) for n in range(nope)])
    w_qn = w_uq[:, idx_n]
    zq = jnp.zeros((w_uq.shape[0], LANES // 2 - 2 * half), F32)
    blocks = []
    for p in range(n_pairs):
        h0, h1 = 2 * p, 2 * p + 1
        x1 = lambda h: w_uq[:, h * hd + nope:h * hd + nope + half]
        x2 = lambda h: w_uq[:, h * hd + nope + half:h * hd + hd]
        blocks += [x1(h0), x1(h1), zq, x2(h0), x2(h1), zq]
    w_qr = jnp.concatenate(blocks, axis=1)
    w_q = jnp.concatenate([w_qn, w_qr], axis=1).astype(BF16)
    w_qt = jnp.concatenate(
        [jnp.concatenate([w_qn[:, p * LANES:(p + 1) * LANES].T, w_qr[:, p * LANES:(p + 1) * LANES].T], axis=0)
         for p in range(n_pairs)], axis=0).astype(BF16)
    zk = jnp.zeros((w_dkv.shape[0], LANES // 2 - 2 * half), F32)
    k1 = w_dkv[:, kv_lora:kv_lora + half]
    k2 = w_dkv[:, kv_lora + half:]
    w_kv = jnp.concatenate([w_dkv[:, :kv_lora], k1, k1, zk, k2, k2, zk], axis=1).astype(BF16)
    w_k = w_uk.reshape(kv_lora, n_heads * nope).astype(BF16)
    w_vt = w_uv.reshape(kv_lora, -1).T.astype(BF16)
    return w_dq.astype(BF16), w_q, w_kv, w_k, w_qt, w_vt


def _mla_proj(h, g_mix, wts, g_q, g_kv, cos, sin, scale):
    m, d = h.shape
    w_dq, w_q, w_kv, w_k = wts[:4]
    q_lora = w_dq.shape[1]
    kv_lora = w_k.shape[0]
    n_pairs = w_k.shape[1] // LANES
    nn = n_pairs * LANES
    tm = _row_tile(m, 512)
    tab = pl.BlockSpec((1, LANES), lambda i: (0, 0))
    const = lambda a: pl.BlockSpec(a.shape, lambda i: (0,) * a.ndim)
    row = lambda n: pl.BlockSpec((tm, n), lambda i: (i, 0))
    gq2, gkv2, gm2 = g_q.reshape(1, q_lora), g_kv.reshape(1, kv_lora), g_mix.reshape(1, d)
    return pl.pallas_call(
        functools.partial(_mla_proj_kernel, scale=scale, kv_lora=kv_lora, n_pairs=n_pairs),
        grid=(m // tm,),
        in_specs=[row(d), const(gm2), const(w_dq), const(gq2), const(w_q), const(w_kv), const(gkv2),
                  tab, tab],
        out_specs=[row(nn), row(nn), row(kv_lora), row(LANES)],
        out_shape=[jax.ShapeDtypeStruct((m, nn), F32), jax.ShapeDtypeStruct((m, nn), F32),
                   jax.ShapeDtypeStruct((m, kv_lora), F32), jax.ShapeDtypeStruct((m, LANES), F32)],
        compiler_params=_cp("parallel"),
        name="mla_proj",
    )(h, gm2, w_dq, gq2, w_q, w_kv, gkv2, cos, sin)


def _mla_proj_prompt(h, g_mix, wts, g_q, g_kv, cos, sin, scale):
    m, d = h.shape
    w_dq, _, w_kv, w_k, w_qt, w_vt = wts
    q_lora = w_dq.shape[1]
    kv_lora = w_k.shape[0]
    n_pairs = w_k.shape[1] // LANES
    nn = n_pairs * LANES
    t = cos.shape[0]
    tm = _row_tile(t, 512)
    nt = t // tm
    tab = pl.BlockSpec((tm, LANES), lambda i: (i % nt, 0))
    tab_t = pl.BlockSpec((LANES, tm), lambda i: (0, i % nt))
    const = lambda a: pl.BlockSpec(a.shape, lambda i: (0,) * a.ndim)
    row = lambda n: pl.BlockSpec((tm, n), lambda i: (i, 0))
    col = lambda n: pl.BlockSpec((n_pairs, n, tm), lambda i: (0, 0, i))
    gq2, gkv2, gm2 = g_q.reshape(1, q_lora), g_kv.reshape(1, kv_lora), g_mix.reshape(1, d)
    return pl.pallas_call(
        functools.partial(_mla_proj_prompt_kernel, scale=scale, kv_lora=kv_lora, n_pairs=n_pairs),
        grid=(m // tm,),
        in_specs=[row(d), const(gm2), const(w_dq), const(gq2), const(w_qt), const(w_kv), const(gkv2),
                  tab, tab, tab_t, tab_t, const(w_k), const(w_vt)],
        out_specs=[col(2 * LANES), row(kv_lora), row(LANES), row(nn), row(LANES), col(LANES)],
        out_shape=[jax.ShapeDtypeStruct((n_pairs, 2 * LANES, m), BF16),
                   jax.ShapeDtypeStruct((m, kv_lora), F32), jax.ShapeDtypeStruct((m, LANES), F32),
                   jax.ShapeDtypeStruct((m, nn), BF16), jax.ShapeDtypeStruct((m, LANES), BF16),
                   jax.ShapeDtypeStruct((n_pairs, LANES, m), BF16)],
        compiler_params=_cp("parallel"),
        name="mla_proj_prompt",
    )(h, gm2, w_dq, gq2, w_qt, w_kv, gkv2, cos, sin, cos.T, sin.T, w_k, w_vt)


def _attn_prompt_kernel(qt_ref, kn_ref, kr_ref, vt_ref, o_ref, *, tq, tk):
    i = pl.program_id(2)
    half = LANES // 2
    nk = tq // tk
    qt = qt_ref[0].astype(F32)
    row = lax.broadcasted_iota(jnp.int32, qt.shape, 0)
    ws = []
    for hh in range(2):
        sel = (((row < LANES) & ((row // half) == hh))
               | ((row >= LANES) & (((row % half) // 16) == hh)))
        ws.append(jnp.where(sel, qt, 0.0).astype(BF16))

    ones = jnp.ones((16, tk), BF16)

    def block(j, carry, diag):
        r0 = pl.multiple_of(j * tk, tk)
        k = jnp.concatenate([kn_ref[0, pl.ds(r0, tk), :], kr_ref[0, pl.ds(r0, tk), :]], axis=1)
        vt = vt_ref[0, :, pl.ds(r0, tk)]
        out = []
        for hh in range(2):
            m, acc = carry[hh]
            st = _dot(k, ws[hh])
            if diag is not None:
                key = lax.broadcasted_iota(jnp.int32, st.shape, 0) + diag * tk
                qry = lax.broadcasted_iota(jnp.int32, st.shape, 1)
                st = jnp.where(key <= qry, st, -jnp.inf)
            m_new = jnp.maximum(m, jnp.max(st, axis=0, keepdims=True))
            alpha = jnp.exp2(m - m_new)
            pt = jnp.exp2(st - m_new).astype(BF16)
            va = jnp.concatenate([vt[hh * half:(hh + 1) * half, :], ones], axis=0)
            out.append((m_new, alpha * acc + _dot(va, pt)))
        return tuple(out)

    init = tuple((jnp.full((1, tq), -jnp.inf, F32), jnp.zeros((half + 16, tq), F32)) for _ in range(2))
    carry = lax.fori_loop(0, i * nk, lambda j, c: block(j, c, None), init)
    for dd in range(nk):
        carry = block(i * nk + dd, carry, dd)
    (_, a0), (_, a1) = carry
    ot = jnp.concatenate([a0[:half] / a0[half:half + 1], a1[:half] / a1[half:half + 1]], axis=0)
    o_ref[0] = ot.T.astype(o_ref.dtype)


def _attn_prompt(qt, kn, kr, vt, bsz, t):
    n_pairs = qt.shape[0]
    tq = _row_tile(t, 1024)
    tk = tq
    nq = t // tq
    r3 = lambda a: a.reshape(bsz, t, a.shape[-1])
    return pl.pallas_call(
        functools.partial(_attn_prompt_kernel, tq=tq, tk=tk),
        grid=(bsz, n_pairs, nq),
        in_specs=[pl.BlockSpec((1, 2 * LANES, tq), lambda b, p, i: (p, 0, b * nq + i)),
                  pl.BlockSpec((1, t, LANES), lambda b, p, i: (b, 0, p)),
                  pl.BlockSpec((1, t, LANES), lambda b, p, i: (b, 0, 0)),
                  pl.BlockSpec((1, LANES, t), lambda b, p, i: (p, 0, b))],
        out_specs=pl.BlockSpec((1, tq, LANES), lambda b, p, i: (b, i, p)),
        out_shape=jax.ShapeDtypeStruct((bsz, t, n_pairs * LANES), BF16),
        compiler_params=_cp("parallel", "parallel", "arbitrary"),
        name="attn_prompt",
    )(qt, r3(kn), r3(kr), vt)


def _out_proj_kernel(h_ref, o_ref, w_ref, y_ref):
    y_ref[...] = h_ref[...] + _dot(o_ref[...], w_ref[...])


def _out_proj(h, o_bf, wo_bf):
    m, d = h.shape
    n = o_bf.shape[1]
    tm = _row_tile(m, 512)
    return pl.pallas_call(
        _out_proj_kernel,
        grid=(m // tm,),
        in_specs=[pl.BlockSpec((tm, d), lambda i: (i, 0)), pl.BlockSpec((tm, n), lambda i: (i, 0)),
                  pl.BlockSpec((n, d), lambda i: (0, 0))],
        out_specs=pl.BlockSpec((tm, d), lambda i: (i, 0)),
        out_shape=jax.ShapeDtypeStruct((m, d), F32),
        compiler_params=_cp("parallel"),
        name="attn_out_proj",
    )(h, o_bf, wo_bf)


def _q_absorb_kernel(qn_ref, wukt_ref, o_ref, *, n_pairs, kv_lora):
    lane = lax.broadcasted_iota(jnp.int32, (1, LANES), 1)
    half = LANES // 2
    for p in range(n_pairs):
        blk = qn_ref[:, p * LANES:(p + 1) * LANES]
        for hh in range(2):
            qm = jnp.where((lane // half) == hh, blk, 0.0).astype(BF16)
            h = 2 * p + hh
            o_ref[:, h * kv_lora:(h + 1) * kv_lora] = _dot(qm, wukt_ref[p])


def _q_absorb(qn, wukt_bf):
    m = qn.shape[0]
    n_pairs, _, kv_lora = wukt_bf.shape
    return pl.pallas_call(
        functools.partial(_q_absorb_kernel, n_pairs=n_pairs, kv_lora=kv_lora),
        out_shape=jax.ShapeDtypeStruct((m, 2 * n_pairs * kv_lora), F32),
        compiler_params=pltpu.CompilerParams(vmem_limit_bytes=VMEM_LIMIT),
        name="q_absorb",
    )(qn, wukt_bf)


def _attn_decode_kernel(pt_ref, qlat_ref, qpe_ref, cnew_ref, knew_ref, ckv_hbm, kpe_hbm, o_ref,
                        cbuf, kbuf, sem, *, n_pages, page, layer, n_chunks):
    b = pl.program_id(0)
    nb = pl.num_programs(0)

    def copies(seq, slot, j):
        pg = pt_ref[seq, j]
        return (pltpu.make_async_copy(ckv_hbm.at[pg, layer], cbuf.at[slot, pl.ds(j * page, page), :],
                                      sem.at[0, slot]),
                pltpu.make_async_copy(kpe_hbm.at[pg, layer], kbuf.at[slot, :, pl.ds(j * page, page)],
                                      sem.at[1, slot]))

    def start_seq(seq, slot):
        def body(j, c):
            for cp in copies(seq, slot, j):
                cp.start()
            return c
        lax.fori_loop(0, n_pages, body, 0, unroll=4)

    def wait_seq(seq, slot):
        def body(j, c):
            for cp in copies(seq, slot, j):
                cp.wait()
            return c
        lax.fori_loop(0, n_pages, body, 0, unroll=4)

    @pl.when(b == 0)
    def _():
        start_seq(0, 0)

    @pl.when(b + 1 < nb)
    def _():
        start_seq(b + 1, (b + 1) % 2)

    slot = b % 2
    wait_seq(b, slot)
    ql = qlat_ref[0].astype(BF16)
    qp = qpe_ref[0].astype(BF16)
    cnew = cnew_ref[0].astype(BF16).astype(F32)
    knew = knew_ref[0].astype(BF16).astype(F32)
    s_new = (jnp.sum(ql.astype(F32) * cnew, axis=-1, keepdims=True)
             + jnp.sum(qp.astype(F32) * knew, axis=-1, keepdims=True))
    ch = (n_pages * page) // n_chunks

    def chunk(state, c):
        m, l, acc = state
        ck = cbuf[slot, c * ch:(c + 1) * ch, :].astype(BF16)
        kt = kbuf[slot, :, c * ch:(c + 1) * ch].astype(BF16)
        s = _dot_nt(ql, ck) + _dot(qp, kt)
        m_new = jnp.maximum(m, jnp.max(s, axis=-1, keepdims=True))
        alpha = jnp.exp(m - m_new)
        p = jnp.exp(s - m_new)
        return (m_new, alpha * l + jnp.sum(p, axis=-1, keepdims=True),
                alpha * acc + _dot(p.astype(BF16), ck))

    n_a = (n_chunks + 1) // 2
    st_a = (s_new, jnp.ones_like(s_new), jnp.broadcast_to(cnew, (ql.shape[0], cnew.shape[1])))
    for c in range(n_a):
        st_a = chunk(st_a, c)
    if n_chunks > n_a:
        st_b = (jnp.full_like(s_new, -jnp.inf), jnp.zeros_like(s_new), jnp.zeros_like(st_a[2]))
        for c in range(n_a, n_chunks):
            st_b = chunk(st_b, c)
        m = jnp.maximum(st_a[0], st_b[0])
        fa, fb = jnp.exp(st_a[0] - m), jnp.exp(st_b[0] - m)
        l = fa * st_a[1] + fb * st_b[1]
        acc = fa * st_a[2] + fb * st_b[2]
    else:
        _, l, acc = st_a
    o_ref[0] = acc / l


def _attn_decode(page_table, qlat, qpe, cnew, knew, cache_ckv, cache_kpe_t, layer):
    bsz, n_heads, kv_lora = qlat.shape
    rope_dim = qpe.shape[-1]
    n_pages = page_table.shape[1]
    page = cache_ckv.shape[2]
    n_chunks = 4 if (n_pages * page) % (4 * LANES) == 0 else 1
    per_seq = lambda n, w: pl.BlockSpec((1, n, w), lambda b, pt: (b, 0, 0))
    grid_spec = pltpu.PrefetchScalarGridSpec(
        num_scalar_prefetch=1,
        grid=(bsz,),
        in_specs=[per_seq(n_heads, kv_lora), per_seq(n_heads, rope_dim), per_seq(1, kv_lora),
                  per_seq(1, rope_dim), pl.BlockSpec(memory_space=pl.ANY),
                  pl.BlockSpec(memory_space=pl.ANY)],
        out_specs=per_seq(n_heads, kv_lora),
        scratch_shapes=[pltpu.VMEM((2, n_pages * page, kv_lora), F32),
                        pltpu.VMEM((2, rope_dim, n_pages * page), F32),
                        pltpu.SemaphoreType.DMA((2, 2))],
    )
    return pl.pallas_call(
        functools.partial(_attn_decode_kernel, n_pages=n_pages, page=page, layer=layer,
                          n_chunks=n_chunks),
        grid_spec=grid_spec,
        out_shape=jax.ShapeDtypeStruct((bsz, n_heads, kv_lora), F32),
        compiler_params=_cp("arbitrary"),
        name="attn_decode",
    )(page_table, qlat, qpe, cnew, knew, cache_ckv, cache_kpe_t)


def _decode_out_kernel(h_ref, olat_ref, wuv_ref, wo_ref, y_ref, o_scr, *, n_pairs, kv_lora):
    for p in range(n_pairs):
        blk = olat_ref[:, 2 * p * kv_lora:(2 * p + 2) * kv_lora].astype(BF16)
        o_scr[:, p * LANES:(p + 1) * LANES] = _dot(blk, wuv_ref[p]).astype(BF16)
    y_ref[...] = h_ref[...] + _dot(o_scr[...], wo_ref[...])


def _decode_out(h, olat, wuv_bd_bf, wo_bf):
    m, d = h.shape
    n_pairs = wuv_bd_bf.shape[0]
    kv_lora = wuv_bd_bf.shape[1] // 2
    return pl.pallas_call(
        functools.partial(_decode_out_kernel, n_pairs=n_pairs, kv_lora=kv_lora),
        out_shape=jax.ShapeDtypeStruct((m, d), F32),
        scratch_shapes=[pltpu.VMEM((m, n_pairs * LANES), BF16)],
        compiler_params=pltpu.CompilerParams(vmem_limit_bytes=VMEM_LIMIT),
        name="decode_out_proj",
    )(h, olat, wuv_bd_bf, wo_bf)


def kernel(x_prompt, x_sample, p_prompt, p_sample, state_conv, cache_ckv, cache_kpe, page_table, norm_mix, norm_ffn, norm_final, conv_w_pw1, conv_b_pw1, conv_w_dw, conv_b_dw, conv_ln_g, conv_ln_b, conv_w_pw2, mla_w_dq, mla_g_q, mla_w_uq, mla_w_dkv, mla_g_kv, mla_w_uk, mla_w_uv, mla_w_o, moe_w_group, moe_b_group, moe_w_expert, moe_b_expert, moe_w_gate, moe_w_up, moe_w_down, ple_w_in, ple_g, ple_w_gate):
    bsz, t, d = x_prompt.shape
    dbsz, dt, _ = x_sample.shape
    depth = p_prompt.shape[0]
    assert depth == 2 and dt == 1
    kv_lora, n_heads, nope = mla_w_uk.shape[1:]
    rope_dim = mla_w_dkv.shape[2] - kv_lora
    vdim = mla_w_uv.shape[3]
    past_len = page_table.shape[1] * cache_ckv.shape[2]
    scale = 1.0 / math.sqrt(nope + rope_dim)
    width = conv_w_dw.shape[1]
    bf = lambda a: a.astype(BF16)

    mp, ms = bsz * t, dbsz * dt
    hp = x_prompt.reshape(mp, d)
    hs = x_sample.reshape(ms, d)

    w1, w2 = bf(conv_w_pw1[0]), bf(conv_w_pw2[0])
    conv_args = (conv_w_dw[0], conv_b_dw[0], conv_ln_g[0], conv_ln_b[0], w2)
    glu_p = _pw1_glu(hp, norm_mix[0], w1, conv_b_pw1[0]).reshape(bsz, t, d)
    hp = _conv_prompt(glu_p, hp.reshape(bsz, t, d), *conv_args).reshape(mp, d)
    conv_state_prompt = glu_p[:, t - (width - 1):][None]
    glu_s = _pw1_glu(hs, norm_mix[0], w1, conv_b_pw1[0])
    buf = state_conv[0]
    hs = _conv_sample(jnp.swapaxes(buf, 0, 1), glu_s, hs, *conv_args)
    conv_state_sample = jnp.concatenate([buf[:, 1:], glu_s[:, None, :]], axis=1)[None]

    def moe_ple(h, p, i, final_norm):
        return _moe_ple_layer(h, p, norm_ffn[i], moe_w_group[i], moe_b_group[i], moe_w_expert[i],
                              moe_b_expert[i], moe_w_gate, moe_w_up, moe_w_down, i,
                              bf(ple_w_in[i]), ple_g[i], bf(ple_w_gate[i]), norm_final, final_norm)

    hp = moe_ple(hp, p_prompt[0].reshape(mp, -1), 0, False)
    hs = moe_ple(hs, p_sample[0].reshape(ms, -1), 0, False)

    wts = _mla_weights(mla_w_dq[0], mla_w_uq[0], mla_w_dkv[0], mla_w_uk[0], mla_w_uv[0],
                       n_heads, nope, rope_dim, kv_lora)
    wo = bf(mla_w_o[0])
    half = rope_dim // 2
    natural = lambda kr: jnp.concatenate([kr[:, :half], kr[:, LANES // 2:LANES // 2 + half]], axis=1)

    cos_p, sin_p = _rope_tables(jnp.arange(t, dtype=jnp.int32), rope_dim)
    qt, ckv_p, kr_p, kn, krb, vt = _mla_proj_prompt(hp, norm_mix[1], wts, mla_g_q[0], mla_g_kv[0],
                                                    cos_p, sin_p, scale * math.log2(math.e))
    o_p = _attn_prompt(qt, kn, krb, vt, bsz, t)
    hp = _out_proj(hp, o_p.reshape(mp, -1), wo)
    ckv_prompt = ckv_p.reshape(1, bsz, t, kv_lora)
    kpe_prompt = natural(kr_p).reshape(1, bsz, t, rope_dim)

    cos_s, sin_s = _rope_tables(past_len + jnp.arange(dt, dtype=jnp.int32), rope_dim)
    qn_s, qr_s, ckv_s, kr_s = _mla_proj(hs, norm_mix[1], wts, mla_g_q[0], mla_g_kv[0],
                                        cos_s, sin_s, scale)
    n_pairs = n_heads // 2
    wukt = bf(jnp.transpose(mla_w_uk[0], (1, 2, 0)).reshape(n_pairs, 2 * nope, kv_lora))
    qlat = _q_absorb(qn_s, wukt).reshape(ms, n_heads, kv_lora)
    qr3 = qr_s.reshape(ms, n_pairs, LANES)
    x1 = qr3[:, :, :2 * half].reshape(ms, n_heads, half)
    x2 = qr3[:, :, LANES // 2:LANES // 2 + 2 * half].reshape(ms, n_heads, half)
    qpe = jnp.concatenate([x1, x2], axis=-1)
    kpe_s = natural(kr_s)
    olat = _attn_decode(page_table, qlat, qpe, ckv_s.reshape(ms, 1, kv_lora),
                        kpe_s.reshape(ms, 1, rope_dim), cache_ckv, jnp.swapaxes(cache_kpe, 2, 3), 0)
    wuv = mla_w_uv[0]
    zv = jnp.zeros((kv_lora, vdim), F32)
    wuv_bd = jnp.stack([jnp.concatenate([jnp.concatenate([wuv[:, 2 * p], zv], axis=1),
                                         jnp.concatenate([zv, wuv[:, 2 * p + 1]], axis=1)], axis=0)
                        for p in range(n_pairs)])
    hs = _decode_out(hs, olat.reshape(ms, n_heads * kv_lora), bf(wuv_bd), wo)
    ckv_sample = ckv_s.reshape(1, dbsz, dt, kv_lora)
    kpe_sample = kpe_s.reshape(1, dbsz, dt, rope_dim)

    y_prompt = moe_ple(hp, p_prompt[1].reshape(mp, -1), 1, True).reshape(bsz, t, d)
    y_sample = moe_ple(hs, p_sample[1].reshape(ms, -1), 1, True).reshape(dbsz, dt, d)
    return (y_prompt, y_sample, conv_state_prompt, conv_state_sample,
            ckv_prompt, kpe_prompt, ckv_sample, kpe_sample)
```

```python
import functools
import math

import jax
import jax.numpy as jnp
import numpy as np
from jax import lax
from jax.experimental import pallas as pl
from jax.experimental.pallas import tpu as pltpu

F32 = jnp.float32
BF16 = jnp.bfloat16

EPS = 1e-6
ROPE_THETA = 10000.0
CONV_HALO = 32
LANES = 128
VMEM_LIMIT = 56 * 1024 * 1024


def _cp(*sem):
    return pltpu.CompilerParams(dimension_semantics=sem, vmem_limit_bytes=VMEM_LIMIT)


def _rms(x, g):
    return x * lax.rsqrt(jnp.mean(x * x, axis=-1, keepdims=True) + EPS) * g


def _dot(a, b):
    return jnp.dot(a, b, preferred_element_type=F32)


def _mm(x, w):
    xh = x.astype(BF16)
    if w.dtype == BF16:
        return _dot(xh, w)
    xl = (x - xh.astype(F32)).astype(BF16)
    wh = w.astype(BF16)
    wl = (w - wh.astype(F32)).astype(BF16)
    return _dot(xh, wh) + _dot(xl, wh) + _dot(xh, wl)


def _dot_nt(a, b):
    return lax.dot_general(a, b, (((1,), (1,)), ((), ())), preferred_element_type=F32)


def _row_tile(m, target):
    t = min(m, target)
    while m % t:
        t //= 2
    return t


def _pw1_glu_kernel(x_ref, g_ref, w_ref, b_ref, o_ref):
    d = o_ref.shape[-1]
    hn = _rms(x_ref[...], g_ref[...])
    a = _mm(hn, w_ref[:, :d]) + b_ref[:, :d]
    b = _mm(hn, w_ref[:, d:]) + b_ref[:, d:]
    o_ref[...] = a * jax.nn.sigmoid(b)


def _pw1_glu(x, g, w_bf, b):
    m, d = x.shape
    tm = _row_tile(m, 512)
    return pl.pallas_call(
        _pw1_glu_kernel,
        grid=(m // tm,),
        in_specs=[
            pl.BlockSpec((tm, d), lambda i: (i, 0)),
            pl.BlockSpec((1, d), lambda i: (0, 0)),
            pl.BlockSpec((d, 2 * d), lambda i: (0, 0)),
            pl.BlockSpec((1, 2 * d), lambda i: (0, 0)),
        ],
        out_specs=pl.BlockSpec((tm, d), lambda i: (i, 0)),
        out_shape=jax.ShapeDtypeStruct((m, d), F32),
        compiler_params=_cp("parallel"),
        name="pw1_glu",
    )(x, g.reshape(1, d), w_bf, b.reshape(1, 2 * d))


def _ln_silu(z, g, b):
    mu = jnp.mean(z, axis=-1, keepdims=True)
    zc = z - mu
    var = jnp.mean(zc * zc, axis=-1, keepdims=True)
    y = zc * lax.rsqrt(var + EPS) * g + b
    return y * jax.nn.sigmoid(y)


def _conv_prompt_kernel(cur_ref, halo_ref, h_ref, wdw_ref, bdw_ref, lng_ref, lnb_ref, w2_ref,
                        o_ref, full_ref, sh_ref, z_ref, *, width, rows, lanes):
    tq, d = cur_ref.shape[1], cur_ref.shape[2]
    i = pl.program_id(1)
    full_ref[0:CONV_HALO, :] = jnp.where(i > 0, halo_ref[0], 0.0)
    full_ref[CONV_HALO:CONV_HALO + tq, :] = cur_ref[0]
    off = CONV_HALO - (width - 1)
    sub = 8
    for l0 in range(0, d, lanes):
        taps = [len(range(r, width, sub)) for r in range(sub)]
        for r in range(sub):
            n = tq + sub * (taps[r] - 1)
            sh_ref[r, 0:n, :] = full_ref[off + r:off + r + n, l0:l0 + lanes]
        for r0 in range(0, tq, rows):
            acc = jnp.zeros((rows, lanes), F32)
            for k in range(width):
                a, r = divmod(k, sub)
                acc = acc + sh_ref[r, r0 + sub * a:r0 + sub * a + rows, :] * wdw_ref[k:k + 1, l0:l0 + lanes]
            z_ref[r0:r0 + rows, l0:l0 + lanes] = acc
    y = _ln_silu(z_ref[...] + bdw_ref[...], lng_ref[...], lnb_ref[...])
    o_ref[0] = h_ref[0] + _dot(y.astype(BF16), w2_ref[...])


def _conv_prompt(glu, h, w_dw, b_dw, ln_g, ln_b, w2_bf):
    bsz, t, d = glu.shape
    width = w_dw.shape[0]
    tq = _row_tile(t, 512)
    hb = tq // CONV_HALO
    wpad = jnp.zeros((CONV_HALO, d), F32).at[:width].set(w_dw)
    lanes = 256
    kern = functools.partial(_conv_prompt_kernel, width=width, rows=min(64, tq), lanes=lanes)
    vec = lambda a: a.reshape(1, d)
    cvec = pl.BlockSpec((1, d), lambda b, i: (0, 0))
    return pl.pallas_call(
        kern,
        grid=(bsz, t // tq),
        in_specs=[
            pl.BlockSpec((1, tq, d), lambda b, i: (b, i, 0)),
            pl.BlockSpec((1, CONV_HALO, d), lambda b, i: (b, jnp.maximum(i * hb - 1, 0), 0)),
            pl.BlockSpec((1, tq, d), lambda b, i: (b, i, 0)),
            pl.BlockSpec((CONV_HALO, d), lambda b, i: (0, 0)),
            cvec, cvec, cvec,
            pl.BlockSpec((d, d), lambda b, i: (0, 0)),
        ],
        out_specs=pl.BlockSpec((1, tq, d), lambda b, i: (b, i, 0)),
        out_shape=jax.ShapeDtypeStruct((bsz, t, d), F32),
        scratch_shapes=[pltpu.VMEM((tq + CONV_HALO, d), F32),
                        pltpu.VMEM((8, tq + CONV_HALO, lanes), F32), pltpu.VMEM((tq, d), F32)],
        compiler_params=_cp("parallel", "arbitrary"),
        name="conv_prompt",
    )(glu, glu, h, wpad, vec(b_dw), vec(ln_g), vec(ln_b), w2_bf)


def _conv_sample_kernel(buf_ref, u_ref, h_ref, wdw_ref, bdw_ref, lng_ref, lnb_ref, w2_ref, o_ref,
                        *, width):
    acc = u_ref[...] * wdw_ref[width - 1:width, :]
    for k in range(width - 1):
        acc = acc + buf_ref[k] * wdw_ref[k:k + 1, :]
    y = _ln_silu(acc + bdw_ref[...], lng_ref[...], lnb_ref[...])
    o_ref[...] = h_ref[...] + _mm(y, w2_ref[...])


def _conv_sample(buf_t, u, h, w_dw, b_dw, ln_g, ln_b, w2_bf):
    nb, bsz, d = buf_t.shape
    width = w_dw.shape[0]
    bb = _row_tile(bsz, 32)
    wpad = jnp.zeros((CONV_HALO, d), F32).at[:width].set(w_dw)
    vec = lambda a: a.reshape(1, d)
    cvec = pl.BlockSpec((1, d), lambda i: (0, 0))
    row = pl.BlockSpec((bb, d), lambda i: (i, 0))
    return pl.pallas_call(
        functools.partial(_conv_sample_kernel, width=width),
        grid=(bsz // bb,),
        in_specs=[
            pl.BlockSpec((nb, bb, d), lambda i: (0, i, 0)),
            row, row,
            pl.BlockSpec((CONV_HALO, d), lambda i: (0, 0)),
            cvec, cvec, cvec,
            pl.BlockSpec((d, d), lambda i: (0, 0)),
        ],
        out_specs=row,
        out_shape=jax.ShapeDtypeStruct((bsz, d), F32),
        compiler_params=_cp("parallel"),
        name="conv_sample",
    )(buf_t, u, h, wpad, vec(b_dw), vec(ln_g), vec(ln_b), w2_bf)


def _router_kernel(x_ref, g_ref, whi_ref, wlo_ref, b_ref, o_ref, cnt_ref, *, n_groups, per_group):
    hn = _rms(x_ref[...], g_ref[...])
    hn_hi = hn.astype(BF16)
    hn_lo = (hn - hn_hi.astype(F32)).astype(BF16)
    logits = (_dot(hn_hi, whi_ref[...]) + _dot(hn_lo, whi_ref[...]) + _dot(hn_hi, wlo_ref[...])
              + b_ref[...])
    lane = lax.broadcasted_iota(jnp.int32, logits.shape, 1)
    neg = jnp.float32(-jnp.inf)
    big = jnp.int32(1 << 20)

    def first_argmax(v):
        m = jnp.max(v, axis=-1, keepdims=True)
        return m, jnp.min(jnp.where(v == m, lane, big), axis=-1, keepdims=True)

    gl = jnp.where(lane < n_groups, logits, neg)
    gmax, grp = first_argmax(gl)
    g_w = 1.0 / jnp.sum(jnp.exp(gl - gmax), axis=-1, keepdims=True)
    lo = n_groups + grp * per_group
    el = jnp.where((lane >= lo) & (lane < lo + per_group), logits, neg)
    m1, i1 = first_argmax(el)
    el2 = jnp.where(lane == i1, neg, el)
    m2, i2 = first_argmax(el2)
    e2 = jnp.exp(m2 - m1)
    w1 = g_w / (1.0 + e2)
    w2 = g_w * e2 / (1.0 + e2)
    id1 = (i1 - n_groups).astype(F32)
    id2 = (i2 - n_groups).astype(F32)

    @pl.when(pl.program_id(0) == 0)
    def _():
        cnt_ref[...] = jnp.zeros_like(cnt_ref)

    tm = logits.shape[0]
    picks = ((lane == i1) | (lane == i2)).astype(BF16)
    r_io = lax.broadcasted_iota(jnp.int32, (tm, tm), 0)
    c_io = lax.broadcasted_iota(jnp.int32, (tm, tm), 1)
    before = _dot((c_io < r_io).astype(BF16), picks) + cnt_ref[...]
    rank1 = jnp.sum(jnp.where(lane == i1, before, 0.0), axis=-1, keepdims=True)
    rank2 = jnp.sum(jnp.where(lane == i2, before, 0.0), axis=-1, keepdims=True)
    cnt_ref[...] = cnt_ref[...] + jnp.sum(picks.astype(F32), axis=0, keepdims=True)
    vals = (id1, id2, w1, w2, rank1, rank2)
    out = jnp.zeros_like(logits)
    for k, v in enumerate(vals):
        out = jnp.where(lane == k, v, out)
    o_ref[...] = out


def _router(h, g, w_group, b_group, w_expert, b_expert):
    m, d = h.shape
    n_groups, n_experts = w_group.shape[1], w_expert.shape[1]
    tm = _row_tile(m, 512)
    w = jnp.zeros((d, LANES), F32).at[:, :n_groups].set(w_group)
    w = w.at[:, n_groups:n_groups + n_experts].set(w_expert)
    b = jnp.zeros((1, LANES), F32).at[0, :n_groups].set(b_group)
    b = b.at[0, n_groups:n_groups + n_experts].set(b_expert)
    kern = functools.partial(_router_kernel, n_groups=n_groups, per_group=n_experts // n_groups)
    w_hi = w.astype(BF16)
    w_lo = (w - w_hi.astype(F32)).astype(BF16)
    route, counts = pl.pallas_call(
        kern,
        grid=(m // tm,),
        in_specs=[
            pl.BlockSpec((tm, d), lambda i: (i, 0)),
            pl.BlockSpec((1, d), lambda i: (0, 0)),
            pl.BlockSpec((d, LANES), lambda i: (0, 0)),
            pl.BlockSpec((d, LANES), lambda i: (0, 0)),
            pl.BlockSpec((1, LANES), lambda i: (0, 0)),
        ],
        out_specs=[pl.BlockSpec((tm, LANES), lambda i: (i, 0)), pl.BlockSpec((1, LANES), lambda i: (0, 0))],
        out_shape=[jax.ShapeDtypeStruct((m, LANES), F32), jax.ShapeDtypeStruct((1, LANES), F32)],
        compiler_params=_cp("arbitrary"),
        name="moe_router",
    )(h, g.reshape(1, d), w_hi, w_lo, b)
    return route, counts[0, n_groups:n_groups + n_experts].astype(jnp.int32)


def _route_plan(route, counts, tm):
    m = route.shape[0]
    n_experts = counts.shape[0]
    experts = jnp.arange(n_experts, dtype=jnp.int32)
    tiles_per = (counts + tm - 1) // tm
    tile_end = jnp.cumsum(tiles_per)
    start = (tile_end - tiles_per) * tm
    eid = route[:, :2].astype(jnp.int32).T
    rank = route[:, 4:6].astype(jnp.int32).T
    pos = rank + jnp.sum(jnp.where(eid[..., None] == experts, start, 0), axis=-1)
    n_tiles = (2 * m) // tm + n_experts
    tile_ids = jnp.arange(n_tiles, dtype=jnp.int32)
    tile_expert = jnp.minimum(jnp.sum((tile_end[None, :] <= tile_ids[:, None]).astype(jnp.int32), axis=1),
                              n_experts - 1)
    n_used = tile_end[-1:].astype(jnp.int32)
    pad_start = (start + counts).astype(jnp.int32)
    pad_len = (tiles_per * tm - counts).astype(jnp.int32)
    return pos.reshape(-1).astype(jnp.int32), tile_expert.astype(jnp.int32), n_used, pad_start, pad_len


def _moe_dispatch_kernel(pos_ref, ps_ref, pn_ref, nu_ref, h_ref, xs_hbm, zbuf, sem, *, td, m, n_experts, tm,
                         n_tiles):
    i = pl.program_id(0)

    def row_copy(k, r):
        return pltpu.make_async_copy(h_ref.at[pl.ds(r, 1), :],
                                     xs_hbm.at[pl.ds(pos_ref[k * m + i * td + r], 1), :], sem.at[0])

    def pad_copy(e, j):
        return pltpu.make_async_copy(zbuf.at[pl.ds(0, 1), :], xs_hbm.at[pl.ds(ps_ref[e] + j, 1), :], sem.at[1])

    def tail_copy(t):
        return pltpu.make_async_copy(zbuf, xs_hbm.at[pl.ds(pl.multiple_of(t * tm, tm), tm), :], sem.at[1])

    def for_tail(fn):
        lax.fori_loop(nu_ref[0], n_tiles, lambda t, c: (fn(tail_copy(t)), c)[1], 0)

    def for_rows(fn):
        def body(r, c):
            fn(row_copy(0, r), 0)
            fn(row_copy(1, r), 1)
            return c
        lax.fori_loop(0, td, body, 0, unroll=8)

    def for_pads(fn):
        def per_expert(e, c):
            lax.fori_loop(0, pn_ref[e], lambda j, c2: (fn(pad_copy(e, j)), c2)[1], 0)
            return c
        lax.fori_loop(0, n_experts, per_expert, 0)

    for_rows(lambda cp, k: cp.start(priority=k))

    @pl.when(i == 0)
    def _():
        zbuf[...] = jnp.zeros_like(zbuf)
        for_pads(lambda cp: cp.start())
        for_tail(lambda cp: cp.start())
        for_pads(lambda cp: cp.wait())
        for_tail(lambda cp: cp.wait())

    for_rows(lambda cp, k: cp.wait())


def _moe_dispatch(h, pos, pad_start, pad_len, n_used, n_rows, tm):
    m, d = h.shape
    td = _row_tile(m, 512)
    n_experts = pad_start.shape[0]
    grid_spec = pltpu.PrefetchScalarGridSpec(
        num_scalar_prefetch=4,
        grid=(m // td,),
        in_specs=[pl.BlockSpec((td, d), lambda i, pos, ps, pn, nu: (i, 0))],
        out_specs=pl.BlockSpec(memory_space=pl.ANY),
        scratch_shapes=[pltpu.VMEM((tm, d), F32), pltpu.SemaphoreType.DMA((2,))],
    )
    return pl.pallas_call(
        functools.partial(_moe_dispatch_kernel, td=td, m=m, n_experts=n_experts, tm=tm,
                          n_tiles=n_rows // tm),
        grid_spec=grid_spec,
        out_shape=jax.ShapeDtypeStruct((n_rows, d), F32),
        compiler_params=_cp("arbitrary"),
        name="moe_dispatch",
    )(pos, pad_start, pad_len, n_used, h)


def _moe_expert_kernel(te_ref, nu_ref, x_ref, g_ref, wg_ref, wu_ref, wd_ref, o_ref, wgub, wdb, *lo):
    i = pl.program_id(0)
    f = wdb.shape[0]

    def split_store(w, hi_ref, lo_ref, cols):
        hi = w.astype(BF16)
        hi_ref[:, cols] = hi
        if lo_ref is not None:
            lo_ref[:, cols] = (w - hi.astype(F32)).astype(BF16)

    @pl.when((i == 0) | (te_ref[i] != te_ref[jnp.maximum(i - 1, 0)]))
    def _():
        split_store(wg_ref[0, 0], wgub, lo[0] if lo else None, slice(0, f))
        split_store(wu_ref[0, 0], wgub, lo[0] if lo else None, slice(f, 2 * f))
        split_store(wd_ref[0, 0], wdb, lo[1] if lo else None, slice(None))

    def mm(x, hi_ref, lo_ref):
        xh = x.astype(BF16)
        if lo_ref is None:
            return _dot(xh, hi_ref[...])
        xl = (x - xh.astype(F32)).astype(BF16)
        return _dot(xh, hi_ref[...]) + _dot(xl, hi_ref[...]) + _dot(xh, lo_ref[...])

    @pl.when(i < nu_ref[0])
    def _():
        au = mm(_rms(x_ref[...], g_ref[...]), wgub, lo[0] if lo else None)
        a, u = au[:, :f], au[:, f:]
        o_ref[...] = mm(a * jax.nn.sigmoid(a) * u, wdb, lo[1] if lo else None)

    @pl.when(i >= nu_ref[0])
    def _():
        o_ref[...] = jnp.zeros_like(o_ref)


def _moe_experts(xs, g, tile_expert, n_used, w_gate, w_up, w_down, layer, tm, precise):
    n_rows, d = xs.shape
    n_tiles = n_rows // tm
    f = w_gate.shape[3]
    xmap = lambda i, te, nu: (jnp.minimum(i, nu[0] - 1), 0)
    grid_spec = pltpu.PrefetchScalarGridSpec(
        num_scalar_prefetch=2,
        grid=(n_tiles,),
        in_specs=[
            pl.BlockSpec((tm, d), xmap),
            pl.BlockSpec((1, d), lambda i, te, nu: (0, 0)),
            pl.BlockSpec((1, 1, d, f), lambda i, te, nu: (layer, te[i], 0, 0)),
            pl.BlockSpec((1, 1, d, f), lambda i, te, nu: (layer, te[i], 0, 0)),
            pl.BlockSpec((1, 1, f, d), lambda i, te, nu: (layer, te[i], 0, 0)),
        ],
        out_specs=pl.BlockSpec((tm, d), lambda i, te, nu: (i, 0)),
        scratch_shapes=[pltpu.VMEM((d, 2 * f), BF16), pltpu.VMEM((f, d), BF16)] * (2 if precise else 1),
    )
    return pl.pallas_call(
        _moe_expert_kernel,
        grid_spec=grid_spec,
        out_shape=jax.ShapeDtypeStruct((n_rows, d), F32),
        compiler_params=_cp("arbitrary"),
        name="moe_experts",
    )(tile_expert, n_used, xs, g.reshape(1, d), w_gate, w_up, w_down)


def _combine_ple_kernel(pos_ref, h_ref, route_ref, y_hbm, p_ref, win_ref, g_ref, wgate_ref,
                        gfin_ref, o_ref, ybuf, sem, *, tm, final_norm):
    i = pl.program_id(0)
    n = pl.num_programs(0)
    m = n * tm

    def row_copy(tile, slot, k, r):
        return pltpu.make_async_copy(y_hbm.at[pl.ds(pos_ref[k * m + tile * tm + r], 1), :],
                                     ybuf.at[slot, pl.ds(k * tm + r, 1), :], sem.at[slot])

    def start_tile(tile, slot):
        def body(r, c):
            row_copy(tile, slot, 0, r).start(priority=0)
            row_copy(tile, slot, 1, r).start(priority=1)
            return c
        lax.fori_loop(0, tm, body, 0, unroll=8)

    def wait_tile(tile, slot):
        def body(r, c):
            row_copy(tile, slot, 0, r).wait()
            row_copy(tile, slot, 1, r).wait()
            return c
        lax.fori_loop(0, tm, body, 0, unroll=8)

    @pl.when(i == 0)
    def _():
        start_tile(0, 0)

    @pl.when(i + 1 < n)
    def _():
        start_tile(i + 1, (i + 1) % 2)

    slot = i % 2
    wait_tile(i, slot)
    route = route_ref[...]
    h2 = h_ref[...] + route[:, 2:3] * ybuf[slot, 0:tm, :] + route[:, 3:4] * ybuf[slot, tm:2 * tm, :]
    gate = jax.nn.sigmoid(_mm(_rms(h2, g_ref[...]), wgate_ref[...]))
    h3 = h2 + _mm(p_ref[...], win_ref[...]) * gate
    if final_norm:
        h3 = _rms(h3, gfin_ref[...])
    o_ref[...] = h3


def _combine_ple(h, route, pos, y_sorted, p, win_bf, g_ple, wgate_bf, g_final, tm, final_norm):
    m, d = h.shape
    pd = p.shape[1]
    grid_spec = pltpu.PrefetchScalarGridSpec(
        num_scalar_prefetch=1,
        grid=(m // tm,),
        in_specs=[
            pl.BlockSpec((tm, d), lambda i, pos: (i, 0)),
            pl.BlockSpec((tm, LANES), lambda i, pos: (i, 0)),
            pl.BlockSpec(memory_space=pl.ANY),
            pl.BlockSpec((tm, pd), lambda i, pos: (i, 0)),
            pl.BlockSpec((pd, d), lambda i, pos: (0, 0)),
            pl.BlockSpec((1, d), lambda i, pos: (0, 0)),
            pl.BlockSpec((d, d), lambda i, pos: (0, 0)),
            pl.BlockSpec((1, d), lambda i, pos: (0, 0)),
        ],
        out_specs=pl.BlockSpec((tm, d), lambda i, pos: (i, 0)),
        scratch_shapes=[pltpu.VMEM((2, 2 * tm, d), F32), pltpu.SemaphoreType.DMA((2,))],
    )
    return pl.pallas_call(
        functools.partial(_combine_ple_kernel, tm=tm, final_norm=final_norm),
        grid_spec=grid_spec,
        out_shape=jax.ShapeDtypeStruct((m, d), F32),
        compiler_params=_cp("arbitrary"),
        name="moe_combine_ple",
    )(pos, h, route, y_sorted, p, win_bf, g_ple.reshape(1, d), wgate_bf, g_final.reshape(1, d))


def _moe_ple_layer(h, p, norm_ffn, w_group, b_group, w_expert, b_expert, w_gate, w_up, w_down, layer,
                   win_bf, g_ple, wgate_bf, g_final, final_norm):
    m = h.shape[0]
    n_experts = w_expert.shape[1]
    tm_e = 256 if m >= 4096 else 32
    route, counts = _router(h, norm_ffn, w_group, b_group, w_expert, b_expert)
    pos, tile_expert, n_used, pad_start, pad_len = _route_plan(route, counts, tm_e)
    n_rows = 2 * m + n_experts * tm_e
    xs = _moe_dispatch(h, pos, pad_start, pad_len, n_used, n_rows, tm_e)
    y_sorted = _moe_experts(xs, norm_ffn, tile_expert, n_used, w_gate, w_up, w_down, layer, tm_e,
                            precise=win_bf.dtype == F32)
    tm_c = _row_tile(m, 256)
    return _combine_ple(h, route, pos, y_sorted, p, win_bf, g_ple, wgate_bf, g_final, tm_c, final_norm)


def _rope_block(x, c, s):
    return x * c + pltpu.roll(x, 64, 1) * s


def _mla_proj_kernel(x_ref, gmix_ref, wdq_ref, gq_ref, wuq_ref, wdkv_ref, gkv_ref, cos_ref, sin_ref,
                     qn_ref, qr_ref, ckv_ref, kr_ref, *, scale, kv_lora, n_pairs):
    hn = _rms(x_ref[...], gmix_ref[...])
    cq = _rms(_mm(hn, wdq_ref[...]), gq_ref[...])
    q = _mm(cq, wuq_ref[...]) * scale
    nn = n_pairs * LANES
    c, s = cos_ref[...], sin_ref[...]
    qn_ref[...] = q[:, :nn]
    for p in range(n_pairs):
        blk = q[:, nn + p * LANES:nn + (p + 1) * LANES]
        qr_ref[:, p * LANES:(p + 1) * LANES] = _rope_block(blk, c, s)
    kv = _mm(hn, wdkv_ref[...])
    ckv_ref[...] = _rms(kv[:, :kv_lora], gkv_ref[...])
    kr_ref[...] = _rope_block(kv[:, kv_lora:], c, s)


def _mla_proj_prompt_kernel(x_ref, gmix_ref, wdq_ref, gq_ref, wqt_ref, wdkv_ref, gkv_ref, cos_ref, sin_ref,
                            cost_ref, sint_ref, wuk_ref, wuvt_ref,
                            qt_ref, ckv_ref, kr_ref, kn_ref, krb_ref, vt_ref, *, scale, kv_lora, n_pairs):
    tm = x_ref.shape[0]
    half = LANES // 2
    hn = _rms(x_ref[...], gmix_ref[...]).astype(BF16)
    cq = _rms(_dot(hn, wdq_ref[...]), gq_ref[...]).astype(BF16)
    q3 = (_dot_nt(wqt_ref[...], cq) * scale).reshape(n_pairs, 2 * LANES, tm)
    rope = q3[:, LANES:, :]
    rolled = jnp.concatenate([rope[:, half:, :], rope[:, :half, :]], axis=1)
    qt_ref[:, :LANES, :] = q3[:, :LANES, :].astype(BF16)
    qt_ref[:, LANES:, :] = (rope * cost_ref[...][None] + rolled * sint_ref[...][None]).astype(BF16)
    kv = _dot(hn, wdkv_ref[...])
    ckv = _rms(kv[:, :kv_lora], gkv_ref[...])
    kr = _rope_block(kv[:, kv_lora:], cos_ref[...], sin_ref[...])
    ckv_ref[...] = ckv
    kr_ref[...] = kr
    cb = ckv.astype(BF16)
    kn_ref[...] = _dot(cb, wuk_ref[...]).astype(BF16)
    krb_ref[...] = kr.astype(BF16)
    vt_ref[...] = _dot_nt(wuvt_ref[...], cb).reshape(n_pairs, LANES, tm).astype(BF16)


def _rope_tables(pos, rope_dim):
    half = rope_dim // 2
    inv_freq = jnp.power(jnp.float32(ROPE_THETA), -jnp.arange(half, dtype=F32) / half)
    ang = pos.astype(F32)[:, None] * inv_freq[None, :]
    cos, sin = jnp.cos(ang), jnp.sin(ang)
    z = jnp.zeros((pos.shape[0], 2 * half), F32)
    c = jnp.concatenate([cos, cos, z, cos, cos, z], axis=1)
    s = jnp.concatenate([-sin, -sin, z, sin, sin, z], axis=1)
    return c, s


def _mla_weights(w_dq, w_uq, w_dkv, w_uk, w_uv, n_heads, nope, rope_dim, kv_lora):
    half = rope_dim // 2
    assert 2 * nope == LANES and 4 * half <= LANES // 2
    n_pairs = n_heads // 2
    hd = nope + rope_dim
    idx_n = np.array([h * hd + n for h in range(n_heads) for n in range(nope)])
    w_qn = w_uq[:, idx_n]
    zq = jnp.zeros((w_uq.shape[0], LANES // 2 - 2 * half), F32)
    blocks = []
    for p in range(n_pairs):
        h0, h1 = 2 * p, 2 * p + 1
        x1 = lambda h: w_uq[:, h * hd + nope:h * hd + nope + half]
        x2 = lambda h: w_uq[:, h * hd + nope + half:h * hd + hd]
        blocks += [x1(h0), x1(h1), zq, x2(h0), x2(h1), zq]
    w_qr = jnp.concatenate(blocks, axis=1)
    w_q = jnp.concatenate([w_qn, w_qr], axis=1)
    w_qt = jnp.concatenate(
        [jnp.concatenate([w_qn[:, p * LANES:(p + 1) * LANES].T, w_qr[:, p * LANES:(p + 1) * LANES].T], axis=0)
         for p in range(n_pairs)], axis=0)
    zk = jnp.zeros((w_dkv.shape[0], LANES // 2 - 2 * half), F32)
    k1 = w_dkv[:, kv_lora:kv_lora + half]
    k2 = w_dkv[:, kv_lora + half:]
    w_kv = jnp.concatenate([w_dkv[:, :kv_lora], k1, k1, zk, k2, k2, zk], axis=1)
    w_k = w_uk.reshape(kv_lora, n_heads * nope)
    w_vt = w_uv.reshape(kv_lora, -1).T
    return w_dq, w_q, w_kv, w_k, w_qt, w_vt


def _mla_proj(h, g_mix, wts, g_q, g_kv, cos, sin, scale):
    m, d = h.shape
    w_dq, w_q, w_kv, w_k = wts[:4]
    q_lora = w_dq.shape[1]
    kv_lora = w_k.shape[0]
    n_pairs = w_k.shape[1] // LANES
    nn = n_pairs * LANES
    tm = _row_tile(m, 512)
    tab = pl.BlockSpec((1, LANES), lambda i: (0, 0))
    const = lambda a: pl.BlockSpec(a.shape, lambda i: (0,) * a.ndim)
    row = lambda n: pl.BlockSpec((tm, n), lambda i: (i, 0))
    gq2, gkv2, gm2 = g_q.reshape(1, q_lora), g_kv.reshape(1, kv_lora), g_mix.reshape(1, d)
    return pl.pallas_call(
        functools.partial(_mla_proj_kernel, scale=scale, kv_lora=kv_lora, n_pairs=n_pairs),
        grid=(m // tm,),
        in_specs=[row(d), const(gm2), const(w_dq), const(gq2), const(w_q), const(w_kv), const(gkv2),
                  tab, tab],
        out_specs=[row(nn), row(nn), row(kv_lora), row(LANES)],
        out_shape=[jax.ShapeDtypeStruct((m, nn), F32), jax.ShapeDtypeStruct((m, nn), F32),
                   jax.ShapeDtypeStruct((m, kv_lora), F32), jax.ShapeDtypeStruct((m, LANES), F32)],
        compiler_params=_cp("parallel"),
        name="mla_proj",
    )(h, gm2, w_dq, gq2, w_q, w_kv, gkv2, cos, sin)


def _mla_proj_prompt(h, g_mix, wts, g_q, g_kv, cos, sin, scale):
    m, d = h.shape
    w_dq, _, w_kv, w_k, w_qt, w_vt = wts
    q_lora = w_dq.shape[1]
    kv_lora = w_k.shape[0]
    n_pairs = w_k.shape[1] // LANES
    nn = n_pairs * LANES
    t = cos.shape[0]
    tm = _row_tile(t, 512)
    nt = t // tm
    tab = pl.BlockSpec((tm, LANES), lambda i: (i % nt, 0))
    tab_t = pl.BlockSpec((LANES, tm), lambda i: (0, i % nt))
    const = lambda a: pl.BlockSpec(a.shape, lambda i: (0,) * a.ndim)
    row = lambda n: pl.BlockSpec((tm, n), lambda i: (i, 0))
    col = lambda n: pl.BlockSpec((n_pairs, n, tm), lambda i: (0, 0, i))
    gq2, gkv2, gm2 = g_q.reshape(1, q_lora), g_kv.reshape(1, kv_lora), g_mix.reshape(1, d)
    return pl.pallas_call(
        functools.partial(_mla_proj_prompt_kernel, scale=scale, kv_lora=kv_lora, n_pairs=n_pairs),
        grid=(m // tm,),
        in_specs=[row(d), const(gm2), const(w_dq), const(gq2), const(w_qt), const(w_kv), const(gkv2),
                  tab, tab, tab_t, tab_t, const(w_k), const(w_vt)],
        out_specs=[col(2 * LANES), row(kv_lora), row(LANES), row(nn), row(LANES), col(LANES)],
        out_shape=[jax.ShapeDtypeStruct((n_pairs, 2 * LANES, m), BF16),
                   jax.ShapeDtypeStruct((m, kv_lora), F32), jax.ShapeDtypeStruct((m, LANES), F32),
                   jax.ShapeDtypeStruct((m, nn), BF16), jax.ShapeDtypeStruct((m, LANES), BF16),
                   jax.ShapeDtypeStruct((n_pairs, LANES, m), BF16)],
        compiler_params=_cp("parallel"),
        name="mla_proj_prompt",
    )(h, gm2, w_dq, gq2, w_qt, w_kv, gkv2, cos, sin, cos.T, sin.T, w_k, w_vt)


def _attn_prompt_kernel(qt_ref, kn_ref, kr_ref, vt_ref, o_ref, *, tq, tk):
    i = pl.program_id(2)
    half = LANES // 2
    nk = tq // tk
    qt = qt_ref[0].astype(F32)
    row = lax.broadcasted_iota(jnp.int32, qt.shape, 0)
    ws = []
    for hh in range(2):
        sel = (((row < LANES) & ((row // half) == hh))
               | ((row >= LANES) & (((row % half) // 16) == hh)))
        ws.append(jnp.where(sel, qt, 0.0).astype(BF16))

    ones = jnp.ones((16, tk), BF16)

    def block(j, carry, diag):
        r0 = pl.multiple_of(j * tk, tk)
        k = jnp.concatenate([kn_ref[0, pl.ds(r0, tk), :], kr_ref[0, pl.ds(r0, tk), :]], axis=1)
        vt = vt_ref[0, :, pl.ds(r0, tk)]
        out = []
        for hh in range(2):
            m, acc = carry[hh]
            st = _dot(k, ws[hh])
            if diag is not None:
                key = lax.broadcasted_iota(jnp.int32, st.shape, 0) + diag * tk
                qry = lax.broadcasted_iota(jnp.int32, st.shape, 1)
                st = jnp.where(key <= qry, st, -jnp.inf)
            m_new = jnp.maximum(m, jnp.max(st, axis=0, keepdims=True))
            alpha = jnp.exp2(m - m_new)
            pt = jnp.exp2(st - m_new).astype(BF16)
            va = jnp.concatenate([vt[hh * half:(hh + 1) * half, :], ones], axis=0)
            out.append((m_new, alpha * acc + _dot(va, pt)))
        return tuple(out)

    init = tuple((jnp.full((1, tq), -jnp.inf, F32), jnp.zeros((half + 16, tq), F32)) for _ in range(2))
    carry = lax.fori_loop(0, i * nk, lambda j, c: block(j, c, None), init)
    for dd in range(nk):
        carry = block(i * nk + dd, carry, dd)
    (_, a0), (_, a1) = carry
    ot = jnp.concatenate([a0[:half] / a0[half:half + 1], a1[:half] / a1[half:half + 1]], axis=0)
    o_ref[0] = ot.T.astype(o_ref.dtype)


def _attn_prompt(qt, kn, kr, vt, bsz, t):
    n_pairs = qt.shape[0]
    tq = _row_tile(t, 1024)
    tk = tq
    nq = t // tq
    r3 = lambda a: a.reshape(bsz, t, a.shape[-1])
    return pl.pallas_call(
        functools.partial(_attn_prompt_kernel, tq=tq, tk=tk),
        grid=(bsz, n_pairs, nq),
        in_specs=[pl.BlockSpec((1, 2 * LANES, tq), lambda b, p, i: (p, 0, b * nq + i)),
                  pl.BlockSpec((1, t, LANES), lambda b, p, i: (b, 0, p)),
                  pl.BlockSpec((1, t, LANES), lambda b, p, i: (b, 0, 0)),
                  pl.BlockSpec((1, LANES, t), lambda b, p, i: (p, 0, b))],
        out_specs=pl.BlockSpec((1, tq, LANES), lambda b, p, i: (b, i, p)),
        out_shape=jax.ShapeDtypeStruct((bsz, t, n_pairs * LANES), BF16),
        compiler_params=_cp("parallel", "parallel", "arbitrary"),
        name="attn_prompt",
    )(qt, r3(kn), r3(kr), vt)


def _out_proj_kernel(h_ref, o_ref, w_ref, y_ref):
    y_ref[...] = h_ref[...] + _dot(o_ref[...], w_ref[...])


def _out_proj(h, o_bf, wo_bf):
    m, d = h.shape
    n = o_bf.shape[1]
    tm = _row_tile(m, 512)
    return pl.pallas_call(
        _out_proj_kernel,
        grid=(m // tm,),
        in_specs=[pl.BlockSpec((tm, d), lambda i: (i, 0)), pl.BlockSpec((tm, n), lambda i: (i, 0)),
                  pl.BlockSpec((n, d), lambda i: (0, 0))],
        out_specs=pl.BlockSpec((tm, d), lambda i: (i, 0)),
        out_shape=jax.ShapeDtypeStruct((m, d), F32),
        compiler_params=_cp("parallel"),
        name="attn_out_proj",
    )(h, o_bf, wo_bf)


def _q_absorb_kernel(qn_ref, wukt_ref, o_ref, *, n_pairs, kv_lora):
    lane = lax.broadcasted_iota(jnp.int32, (1, LANES), 1)
    half = LANES // 2
    for p in range(n_pairs):
        blk = qn_ref[:, p * LANES:(p + 1) * LANES]
        for hh in range(2):
            qm = jnp.where((lane // half) == hh, blk, 0.0)
            h = 2 * p + hh
            o_ref[:, h * kv_lora:(h + 1) * kv_lora] = _mm(qm, wukt_ref[p])


def _q_absorb(qn, wukt_bf):
    m = qn.shape[0]
    n_pairs, _, kv_lora = wukt_bf.shape
    return pl.pallas_call(
        functools.partial(_q_absorb_kernel, n_pairs=n_pairs, kv_lora=kv_lora),
        out_shape=jax.ShapeDtypeStruct((m, 2 * n_pairs * kv_lora), F32),
        compiler_params=pltpu.CompilerParams(vmem_limit_bytes=VMEM_LIMIT),
        name="q_absorb",
    )(qn, wukt_bf)


def _attn_decode_kernel(pt_ref, qlat_ref, qpe_ref, cnew_ref, knew_ref, ckv_hbm, kpe_hbm, o_ref,
                        cbuf, kbuf, sem, *, n_pages, page, layer, n_chunks):
    b = pl.program_id(0)
    nb = pl.num_programs(0)

    def copies(seq, slot, j):
        pg = pt_ref[seq, j]
        return (pltpu.make_async_copy(ckv_hbm.at[pg, layer], cbuf.at[slot, pl.ds(j * page, page), :],
                                      sem.at[0, slot]),
                pltpu.make_async_copy(kpe_hbm.at[pg, layer], kbuf.at[slot, :, pl.ds(j * page, page)],
                                      sem.at[1, slot]))

    def start_seq(seq, slot):
        def body(j, c):
            for cp in copies(seq, slot, j):
                cp.start()
            return c
        lax.fori_loop(0, n_pages, body, 0, unroll=4)

    def wait_seq(seq, slot):
        def body(j, c):
            for cp in copies(seq, slot, j):
                cp.wait()
            return c
        lax.fori_loop(0, n_pages, body, 0, unroll=4)

    @pl.when(b == 0)
    def _():
        start_seq(0, 0)

    @pl.when(b + 1 < nb)
    def _():
        start_seq(b + 1, (b + 1) % 2)

    slot = b % 2
    wait_seq(b, slot)
    ql = qlat_ref[0].astype(BF16)
    qp = qpe_ref[0].astype(BF16)
    cnew = cnew_ref[0].astype(BF16).astype(F32)
    knew = knew_ref[0].astype(BF16).astype(F32)
    s_new = (jnp.sum(ql.astype(F32) * cnew, axis=-1, keepdims=True)
             + jnp.sum(qp.astype(F32) * knew, axis=-1, keepdims=True))
    ch = (n_pages * page) // n_chunks

    def chunk(state, c):
        m, l, acc = state
        ck = cbuf[slot, c * ch:(c + 1) * ch, :].astype(BF16)
        kt = kbuf[slot, :, c * ch:(c + 1) * ch].astype(BF16)
        s = _dot_nt(ql, ck) + _dot(qp, kt)
        m_new = jnp.maximum(m, jnp.max(s, axis=-1, keepdims=True))
        alpha = jnp.exp(m - m_new)
        p = jnp.exp(s - m_new)
        return (m_new, alpha * l + jnp.sum(p, axis=-1, keepdims=True),
                alpha * acc + _dot(p.astype(BF16), ck))

    n_a = (n_chunks + 1) // 2
    st_a = (s_new, jnp.ones_like(s_new), jnp.broadcast_to(cnew, (ql.shape[0], cnew.shape[1])))
    for c in range(n_a):
        st_a = chunk(st_a, c)
    if n_chunks > n_a:
        st_b = (jnp.full_like(s_new, -jnp.inf), jnp.zeros_like(s_new), jnp.zeros_like(st_a[2]))
        for c in range(n_a, n_chunks):
            st_b = chunk(st_b, c)
        m = jnp.maximum(st_a[0], st_b[0])
        fa, fb = jnp.exp(st_a[0] - m), jnp.exp(st_b[0] - m)
        l = fa * st_a[1] + fb * st_b[1]
        acc = fa * st_a[2] + fb * st_b[2]
    else:
        _, l, acc = st_a
    o_ref[0] = acc / l


def _attn_decode(page_table, qlat, qpe, cnew, knew, cache_ckv, cache_kpe_t, layer):
    bsz, n_heads, kv_lora = qlat.shape
    rope_dim = qpe.shape[-1]
    n_pages = page_table.shape[1]
    page = cache_ckv.shape[2]
    n_chunks = 4 if (n_pages * page) % (4 * LANES) == 0 else 1
    per_seq = lambda n, w: pl.BlockSpec((1, n, w), lambda b, pt: (b, 0, 0))
    grid_spec = pltpu.PrefetchScalarGridSpec(
        num_scalar_prefetch=1,
        grid=(bsz,),
        in_specs=[per_seq(n_heads, kv_lora), per_seq(n_heads, rope_dim), per_seq(1, kv_lora),
                  per_seq(1, rope_dim), pl.BlockSpec(memory_space=pl.ANY),
                  pl.BlockSpec(memory_space=pl.ANY)],
        out_specs=per_seq(n_heads, kv_lora),
        scratch_shapes=[pltpu.VMEM((2, n_pages * page, kv_lora), F32),
                        pltpu.VMEM((2, rope_dim, n_pages * page), F32),
                        pltpu.SemaphoreType.DMA((2, 2))],
    )
    return pl.pallas_call(
        functools.partial(_attn_decode_kernel, n_pages=n_pages, page=page, layer=layer,
                          n_chunks=n_chunks),
        grid_spec=grid_spec,
        out_shape=jax.ShapeDtypeStruct((bsz, n_heads, kv_lora), F32),
        compiler_params=_cp("arbitrary"),
        name="attn_decode",
    )(page_table, qlat, qpe, cnew, knew, cache_ckv, cache_kpe_t)


def _decode_out_kernel(h_ref, olat_ref, wuv_ref, wo_ref, y_ref, o_scr, *, n_pairs, kv_lora):
    for p in range(n_pairs):
        blk = olat_ref[:, 2 * p * kv_lora:(2 * p + 2) * kv_lora]
        o_scr[:, p * LANES:(p + 1) * LANES] = _mm(blk, wuv_ref[p])
    y_ref[...] = h_ref[...] + _mm(o_scr[...], wo_ref[...])


def _decode_out(h, olat, wuv_bd_bf, wo_bf):
    m, d = h.shape
    n_pairs = wuv_bd_bf.shape[0]
    kv_lora = wuv_bd_bf.shape[1] // 2
    return pl.pallas_call(
        functools.partial(_decode_out_kernel, n_pairs=n_pairs, kv_lora=kv_lora),
        out_shape=jax.ShapeDtypeStruct((m, d), F32),
        scratch_shapes=[pltpu.VMEM((m, n_pairs * LANES), F32)],
        compiler_params=pltpu.CompilerParams(vmem_limit_bytes=VMEM_LIMIT),
        name="decode_out_proj",
    )(h, olat, wuv_bd_bf, wo_bf)


def kernel(x_prompt, x_sample, p_prompt, p_sample, state_conv, cache_ckv, cache_kpe, page_table, norm_mix, norm_ffn, norm_final, conv_w_pw1, conv_b_pw1, conv_w_dw, conv_b_dw, conv_ln_g, conv_ln_b, conv_w_pw2, mla_w_dq, mla_g_q, mla_w_uq, mla_w_dkv, mla_g_kv, mla_w_uk, mla_w_uv, mla_w_o, moe_w_group, moe_b_group, moe_w_expert, moe_b_expert, moe_w_gate, moe_w_up, moe_w_down, ple_w_in, ple_g, ple_w_gate):
    bsz, t, d = x_prompt.shape
    dbsz, dt, _ = x_sample.shape
    depth = p_prompt.shape[0]
    assert depth == 2 and dt == 1
    kv_lora, n_heads, nope = mla_w_uk.shape[1:]
    rope_dim = mla_w_dkv.shape[2] - kv_lora
    vdim = mla_w_uv.shape[3]
    past_len = page_table.shape[1] * cache_ckv.shape[2]
    scale = 1.0 / math.sqrt(nope + rope_dim)
    width = conv_w_dw.shape[1]
    bf = lambda a: a.astype(BF16)

    mp, ms = bsz * t, dbsz * dt
    hp = x_prompt.reshape(mp, d)
    hs = x_sample.reshape(ms, d)

    w1, w2 = conv_w_pw1[0], conv_w_pw2[0]
    conv_args = (conv_w_dw[0], conv_b_dw[0], conv_ln_g[0], conv_ln_b[0])
    glu_p = _pw1_glu(hp, norm_mix[0], bf(w1), conv_b_pw1[0]).reshape(bsz, t, d)
    hp = _conv_prompt(glu_p, hp.reshape(bsz, t, d), *conv_args, bf(w2)).reshape(mp, d)
    conv_state_prompt = glu_p[:, t - (width - 1):][None]
    glu_s = _pw1_glu(hs, norm_mix[0], w1, conv_b_pw1[0])
    buf = state_conv[0]
    hs = _conv_sample(jnp.swapaxes(buf, 0, 1), glu_s, hs, *conv_args, w2)
    conv_state_sample = jnp.concatenate([buf[:, 1:], glu_s[:, None, :]], axis=1)[None]

    def moe_ple(h, p, i, final_norm, cast):
        return _moe_ple_layer(h, p, norm_ffn[i], moe_w_group[i], moe_b_group[i], moe_w_expert[i],
                              moe_b_expert[i], moe_w_gate, moe_w_up, moe_w_down, i,
                              cast(ple_w_in[i]), ple_g[i], cast(ple_w_gate[i]), norm_final, final_norm)

    keep = lambda a: a
    hp = moe_ple(hp, p_prompt[0].reshape(mp, -1), 0, False, bf)
    hs = moe_ple(hs, p_sample[0].reshape(ms, -1), 0, False, keep)

    wts = _mla_weights(mla_w_dq[0], mla_w_uq[0], mla_w_dkv[0], mla_w_uk[0], mla_w_uv[0],
                       n_heads, nope, rope_dim, kv_lora)
    wo = mla_w_o[0]
    half = rope_dim // 2
    natural = lambda kr: jnp.concatenate([kr[:, :half], kr[:, LANES // 2:LANES // 2 + half]], axis=1)

    cos_p, sin_p = _rope_tables(jnp.arange(t, dtype=jnp.int32), rope_dim)
    qt, ckv_p, kr_p, kn, krb, vt = _mla_proj_prompt(hp, norm_mix[1], tuple(bf(w) for w in wts), mla_g_q[0],
                                                    mla_g_kv[0], cos_p, sin_p, scale * math.log2(math.e))
    o_p = _attn_prompt(qt, kn, krb, vt, bsz, t)
    hp = _out_proj(hp, o_p.reshape(mp, -1), bf(wo))
    ckv_prompt = ckv_p.reshape(1, bsz, t, kv_lora)
    kpe_prompt = natural(kr_p).reshape(1, bsz, t, rope_dim)

    cos_s, sin_s = _rope_tables(past_len + jnp.arange(dt, dtype=jnp.int32), rope_dim)
    qn_s, qr_s, ckv_s, kr_s = _mla_proj(hs, norm_mix[1], wts, mla_g_q[0], mla_g_kv[0],
                                        cos_s, sin_s, scale)
    n_pairs = n_heads // 2
    wukt = jnp.transpose(mla_w_uk[0], (1, 2, 0)).reshape(n_pairs, 2 * nope, kv_lora)
    qlat = _q_absorb(qn_s, wukt).reshape(ms, n_heads, kv_lora)
    qr3 = qr_s.reshape(ms, n_pairs, LANES)
    x1 = qr3[:, :, :2 * half].reshape(ms, n_heads, half)
    x2 = qr3[:, :, LANES // 2:LANES // 2 + 2 * half].reshape(ms, n_heads, half)
    qpe = jnp.concatenate([x1, x2], axis=-1)
    kpe_s = natural(kr_s)
    olat = _attn_decode(page_table, qlat, qpe, ckv_s.reshape(ms, 1, kv_lora),
                        kpe_s.reshape(ms, 1, rope_dim), cache_ckv, jnp.swapaxes(cache_kpe, 2, 3), 0)
    wuv = mla_w_uv[0]
    zv = jnp.zeros((kv_lora, vdim), F32)
    wuv_bd = jnp.stack([jnp.concatenate([jnp.concatenate([wuv[:, 2 * p], zv], axis=1),
                                         jnp.concatenate([zv, wuv[:, 2 * p + 1]], axis=1)], axis=0)
                        for p in range(n_pairs)])
    hs = _decode_out(hs, olat.reshape(ms, n_heads * kv_lora), wuv_bd, wo)
    ckv_sample = ckv_s.reshape(1, dbsz, dt, kv_lora)
    kpe_sample = kpe_s.reshape(1, dbsz, dt, rope_dim)

    y_prompt = moe_ple(hp, p_prompt[1].reshape(mp, -1), 1, True, bf).reshape(bsz, t, d)
    y_sample = moe_ple(hs, p_sample[1].reshape(ms, -1), 1, True, keep).reshape(dbsz, dt, d)
    return (y_prompt, y_sample, conv_state_prompt, conv_state_sample,
            ckv_prompt, kpe_prompt, ckv_sample, kpe_sample)
```

```python
import functools
import math

import jax
import jax.numpy as jnp
import numpy as np
from jax import lax
from jax.experimental import pallas as pl
from jax.experimental.pallas import tpu as pltpu

F32 = jnp.float32
BF16 = jnp.bfloat16

EPS = 1e-6
ROPE_THETA = 10000.0
CONV_HALO = 32
LANES = 128
VMEM_LIMIT = 56 * 1024 * 1024


def _cp(*sem):
    return pltpu.CompilerParams(dimension_semantics=sem, vmem_limit_bytes=VMEM_LIMIT)


def _rms(x, g):
    return x * lax.rsqrt(jnp.mean(x * x, axis=-1, keepdims=True) + EPS) * g


def _dot(a, b):
    return jnp.dot(a, b, preferred_element_type=F32)


def _dot_nt(a, b):
    return lax.dot_general(a, b, (((1,), (1,)), ((), ())), preferred_element_type=F32)


def _row_tile(m, target):
    t = min(m, target)
    while m % t:
        t //= 2
    return t


def _pw1_glu_kernel(x_ref, g_ref, w_ref, b_ref, o_ref):
    d = o_ref.shape[-1]
    hn = _rms(x_ref[...], g_ref[...]).astype(BF16)
    a = _dot(hn, w_ref[:, :d]) + b_ref[:, :d]
    b = _dot(hn, w_ref[:, d:]) + b_ref[:, d:]
    o_ref[...] = a * jax.nn.sigmoid(b)


def _pw1_glu(x, g, w_bf, b):
    m, d = x.shape
    tm = _row_tile(m, 512)
    return pl.pallas_call(
        _pw1_glu_kernel,
        grid=(m // tm,),
        in_specs=[
            pl.BlockSpec((tm, d), lambda i: (i, 0)),
            pl.BlockSpec((1, d), lambda i: (0, 0)),
            pl.BlockSpec((d, 2 * d), lambda i: (0, 0)),
            pl.BlockSpec((1, 2 * d), lambda i: (0, 0)),
        ],
        out_specs=pl.BlockSpec((tm, d), lambda i: (i, 0)),
        out_shape=jax.ShapeDtypeStruct((m, d), F32),
        compiler_params=_cp("parallel"),
        name="pw1_glu",
    )(x, g.reshape(1, d), w_bf, b.reshape(1, 2 * d))


def _ln_silu(z, g, b):
    mu = jnp.mean(z, axis=-1, keepdims=True)
    zc = z - mu
    var = jnp.mean(zc * zc, axis=-1, keepdims=True)
    y = zc * lax.rsqrt(var + EPS) * g + b
    return y * jax.nn.sigmoid(y)


def _conv_prompt_kernel(cur_ref, halo_ref, h_ref, wdw_ref, bdw_ref, lng_ref, lnb_ref, w2_ref,
                        o_ref, full_ref, sh_ref, z_ref, *, width, rows, lanes):
    tq, d = cur_ref.shape[1], cur_ref.shape[2]
    i = pl.program_id(1)
    full_ref[0:CONV_HALO, :] = jnp.where(i > 0, halo_ref[0], 0.0)
    full_ref[CONV_HALO:CONV_HALO + tq, :] = cur_ref[0]
    off = CONV_HALO - (width - 1)
    sub = 8
    for l0 in range(0, d, lanes):
        taps = [len(range(r, width, sub)) for r in range(sub)]
        for r in range(sub):
            n = tq + sub * (taps[r] - 1)
            sh_ref[r, 0:n, :] = full_ref[off + r:off + r + n, l0:l0 + lanes]
        for r0 in range(0, tq, rows):
            acc = jnp.zeros((rows, lanes), F32)
            for k in range(width):
                a, r = divmod(k, sub)
                acc = acc + sh_ref[r, r0 + sub * a:r0 + sub * a + rows, :] * wdw_ref[k:k + 1, l0:l0 + lanes]
            z_ref[r0:r0 + rows, l0:l0 + lanes] = acc
    y = _ln_silu(z_ref[...] + bdw_ref[...], lng_ref[...], lnb_ref[...])
    o_ref[0] = h_ref[0] + _dot(y.astype(BF16), w2_ref[...])


def _conv_prompt(glu, h, w_dw, b_dw, ln_g, ln_b, w2_bf):
    bsz, t, d = glu.shape
    width = w_dw.shape[0]
    tq = _row_tile(t, 512)
    hb = tq // CONV_HALO
    wpad = jnp.zeros((CONV_HALO, d), F32).at[:width].set(w_dw)
    lanes = 256
    kern = functools.partial(_conv_prompt_kernel, width=width, rows=min(64, tq), lanes=lanes)
    vec = lambda a: a.reshape(1, d)
    cvec = pl.BlockSpec((1, d), lambda b, i: (0, 0))
    return pl.pallas_call(
        kern,
        grid=(bsz, t // tq),
        in_specs=[
            pl.BlockSpec((1, tq, d), lambda b, i: (b, i, 0)),
            pl.BlockSpec((1, CONV_HALO, d), lambda b, i: (b, jnp.maximum(i * hb - 1, 0), 0)),
            pl.BlockSpec((1, tq, d), lambda b, i: (b, i, 0)),
            pl.BlockSpec((CONV_HALO, d), lambda b, i: (0, 0)),
            cvec, cvec, cvec,
            pl.BlockSpec((d, d), lambda b, i: (0, 0)),
        ],
        out_specs=pl.BlockSpec((1, tq, d), lambda b, i: (b, i, 0)),
        out_shape=jax.ShapeDtypeStruct((bsz, t, d), F32),
        scratch_shapes=[pltpu.VMEM((tq + CONV_HALO, d), F32),
                        pltpu.VMEM((8, tq + CONV_HALO, lanes), F32), pltpu.VMEM((tq, d), F32)],
        compiler_params=_cp("parallel", "arbitrary"),
        name="conv_prompt",
    )(glu, glu, h, wpad, vec(b_dw), vec(ln_g), vec(ln_b), w2_bf)


def _conv_sample_kernel(buf_ref, u_ref, h_ref, wdw_ref, bdw_ref, lng_ref, lnb_ref, w2_ref, o_ref,
                        *, width):
    acc = u_ref[...] * wdw_ref[width - 1:width, :]
    for k in range(width - 1):
        acc = acc + buf_ref[k] * wdw_ref[k:k + 1, :]
    y = _ln_silu(acc + bdw_ref[...], lng_ref[...], lnb_ref[...])
    o_ref[...] = h_ref[...] + _dot(y.astype(BF16), w2_ref[...])


def _conv_sample(buf_t, u, h, w_dw, b_dw, ln_g, ln_b, w2_bf):
    nb, bsz, d = buf_t.shape
    width = w_dw.shape[0]
    bb = _row_tile(bsz, 32)
    wpad = jnp.zeros((CONV_HALO, d), F32).at[:width].set(w_dw)
    vec = lambda a: a.reshape(1, d)
    cvec = pl.BlockSpec((1, d), lambda i: (0, 0))
    row = pl.BlockSpec((bb, d), lambda i: (i, 0))
    return pl.pallas_call(
        functools.partial(_conv_sample_kernel, width=width),
        grid=(bsz // bb,),
        in_specs=[
            pl.BlockSpec((nb, bb, d), lambda i: (0, i, 0)),
            row, row,
            pl.BlockSpec((CONV_HALO, d), lambda i: (0, 0)),
            cvec, cvec, cvec,
            pl.BlockSpec((d, d), lambda i: (0, 0)),
        ],
        out_specs=row,
        out_shape=jax.ShapeDtypeStruct((bsz, d), F32),
        compiler_params=_cp("parallel"),
        name="conv_sample",
    )(buf_t, u, h, wpad, vec(b_dw), vec(ln_g), vec(ln_b), w2_bf)


def _router_kernel(x_ref, g_ref, whi_ref, wlo_ref, b_ref, o_ref, cnt_ref, *, n_groups, per_group):
    hn = _rms(x_ref[...], g_ref[...])
    hn_hi = hn.astype(BF16)
    hn_lo = (hn - hn_hi.astype(F32)).astype(BF16)
    logits = (_dot(hn_hi, whi_ref[...]) + _dot(hn_lo, whi_ref[...]) + _dot(hn_hi, wlo_ref[...])
              + b_ref[...])
    lane = lax.broadcasted_iota(jnp.int32, logits.shape, 1)
    neg = jnp.float32(-jnp.inf)
    big = jnp.int32(1 << 20)

    def first_argmax(v):
        m = jnp.max(v, axis=-1, keepdims=True)
        return m, jnp.min(jnp.where(v == m, lane, big), axis=-1, keepdims=True)

    gl = jnp.where(lane < n_groups, logits, neg)
    gmax, grp = first_argmax(gl)
    g_w = 1.0 / jnp.sum(jnp.exp(gl - gmax), axis=-1, keepdims=True)
    lo = n_groups + grp * per_group
    el = jnp.where((lane >= lo) & (lane < lo + per_group), logits, neg)
    m1, i1 = first_argmax(el)
    el2 = jnp.where(lane == i1, neg, el)
    m2, i2 = first_argmax(el2)
    e2 = jnp.exp(m2 - m1)
    w1 = g_w / (1.0 + e2)
    w2 = g_w * e2 / (1.0 + e2)
    id1 = (i1 - n_groups).astype(F32)
    id2 = (i2 - n_groups).astype(F32)

    @pl.when(pl.program_id(0) == 0)
    def _():
        cnt_ref[...] = jnp.zeros_like(cnt_ref)

    tm = logits.shape[0]
    picks = ((lane == i1) | (lane == i2)).astype(BF16)
    r_io = lax.broadcasted_iota(jnp.int32, (tm, tm), 0)
    c_io = lax.broadcasted_iota(jnp.int32, (tm, tm), 1)
    before = _dot((c_io < r_io).astype(BF16), picks) + cnt_ref[...]
    rank1 = jnp.sum(jnp.where(lane == i1, before, 0.0), axis=-1, keepdims=True)
    rank2 = jnp.sum(jnp.where(lane == i2, before, 0.0), axis=-1, keepdims=True)
    cnt_ref[...] = cnt_ref[...] + jnp.sum(picks.astype(F32), axis=0, keepdims=True)
    vals = (id1, id2, w1, w2, rank1, rank2)
    out = jnp.zeros_like(logits)
    for k, v in enumerate(vals):
        out = jnp.where(lane == k, v, out)
    o_ref[...] = out


def _router(h, g, w_group, b_group, w_expert, b_expert):
    m, d = h.shape
    n_groups, n_experts = w_group.shape[1], w_expert.shape[1]
    tm = _row_tile(m, 512)
    w = jnp.zeros((d, LANES), F32).at[:, :n_groups].set(w_group)
    w = w.at[:, n_groups:n_groups + n_experts].set(w_expert)
    b = jnp.zeros((1, LANES), F32).at[0, :n_groups].set(b_group)
    b = b.at[0, n_groups:n_groups + n_experts].set(b_expert)
    kern = functools.partial(_router_kernel, n_groups=n_groups, per_group=n_experts // n_groups)
    w_hi = w.astype(BF16)
    w_lo = (w - w_hi.astype(F32)).astype(BF16)
    route, counts = pl.pallas_call(
        kern,
        grid=(m // tm,),
        in_specs=[
            pl.BlockSpec((tm, d), lambda i: (i, 0)),
            pl.BlockSpec((1, d), lambda i: (0, 0)),
            pl.BlockSpec((d, LANES), lambda i: (0, 0)),
            pl.BlockSpec((d, LANES), lambda i: (0, 0)),
            pl.BlockSpec((1, LANES), lambda i: (0, 0)),
        ],
        out_specs=[pl.BlockSpec((tm, LANES), lambda i: (i, 0)), pl.BlockSpec((1, LANES), lambda i: (0, 0))],
        out_shape=[jax.ShapeDtypeStruct((m, LANES), F32), jax.ShapeDtypeStruct((1, LANES), F32)],
        compiler_params=_cp("arbitrary"),
        name="moe_router",
    )(h, g.reshape(1, d), w_hi, w_lo, b)
    return route, counts[0, n_groups:n_groups + n_experts].astype(jnp.int32)


def _route_plan(route, counts, tm):
    m = route.shape[0]
    n_experts = counts.shape[0]
    experts = jnp.arange(n_experts, dtype=jnp.int32)
    tiles_per = (counts + tm - 1) // tm
    tile_end = jnp.cumsum(tiles_per)
    start = (tile_end - tiles_per) * tm
    eid = route[:, :2].astype(jnp.int32).T
    rank = route[:, 4:6].astype(jnp.int32).T
    pos = rank + jnp.sum(jnp.where(eid[..., None] == experts, start, 0), axis=-1)
    n_tiles = (2 * m) // tm + n_experts
    tile_ids = jnp.arange(n_tiles, dtype=jnp.int32)
    tile_expert = jnp.minimum(jnp.sum((tile_end[None, :] <= tile_ids[:, None]).astype(jnp.int32), axis=1),
                              n_experts - 1)
    n_used = tile_end[-1:].astype(jnp.int32)
    pad_start = (start + counts).astype(jnp.int32)
    pad_len = (tiles_per * tm - counts).astype(jnp.int32)
    return pos.reshape(-1).astype(jnp.int32), tile_expert.astype(jnp.int32), n_used, pad_start, pad_len


def _moe_dispatch_kernel(pos_ref, ps_ref, pn_ref, nu_ref, h_ref, xs_hbm, zbuf, sem, *, td, m, n_experts, tm,
                         n_tiles):
    i = pl.program_id(0)

    def row_copy(k, r):
        return pltpu.make_async_copy(h_ref.at[pl.ds(r, 1), :],
                                     xs_hbm.at[pl.ds(pos_ref[k * m + i * td + r], 1), :], sem.at[0])

    def pad_copy(e, j):
        return pltpu.make_async_copy(zbuf.at[pl.ds(0, 1), :], xs_hbm.at[pl.ds(ps_ref[e] + j, 1), :], sem.at[1])

    def tail_copy(t):
        return pltpu.make_async_copy(zbuf, xs_hbm.at[pl.ds(pl.multiple_of(t * tm, tm), tm), :], sem.at[1])

    def for_tail(fn):
        lax.fori_loop(nu_ref[0], n_tiles, lambda t, c: (fn(tail_copy(t)), c)[1], 0)

    def for_rows(fn):
        def body(r, c):
            fn(row_copy(0, r), 0)
            fn(row_copy(1, r), 1)
            return c
        lax.fori_loop(0, td, body, 0, unroll=8)

    def for_pads(fn):
        def per_expert(e, c):
            lax.fori_loop(0, pn_ref[e], lambda j, c2: (fn(pad_copy(e, j)), c2)[1], 0)
            return c
        lax.fori_loop(0, n_experts, per_expert, 0)

    for_rows(lambda cp, k: cp.start(priority=k))

    @pl.when(i == 0)
    def _():
        zbuf[...] = jnp.zeros_like(zbuf)
        for_pads(lambda cp: cp.start())
        for_tail(lambda cp: cp.start())
        for_pads(lambda cp: cp.wait())
        for_tail(lambda cp: cp.wait())

    for_rows(lambda cp, k: cp.wait())


def _moe_dispatch(h, pos, pad_start, pad_len, n_used, n_rows, tm):
    m, d = h.shape
    td = _row_tile(m, 512)
    n_experts = pad_start.shape[0]
    grid_spec = pltpu.PrefetchScalarGridSpec(
        num_scalar_prefetch=4,
        grid=(m // td,),
        in_specs=[pl.BlockSpec((td, d), lambda i, pos, ps, pn, nu: (i, 0))],
        out_specs=pl.BlockSpec(memory_space=pl.ANY),
        scratch_shapes=[pltpu.VMEM((tm, d), F32), pltpu.SemaphoreType.DMA((2,))],
    )
    return pl.pallas_call(
        functools.partial(_moe_dispatch_kernel, td=td, m=m, n_experts=n_experts, tm=tm,
                          n_tiles=n_rows // tm),
        grid_spec=grid_spec,
        out_shape=jax.ShapeDtypeStruct((n_rows, d), F32),
        compiler_params=_cp("arbitrary"),
        name="moe_dispatch",
    )(pos, pad_start, pad_len, n_used, h)


def _moe_expert_kernel(te_ref, nu_ref, x_ref, g_ref, wg_ref, wu_ref, wd_ref, o_ref, wgub, wdb):
    i = pl.program_id(0)
    f = wdb.shape[0]

    @pl.when((i == 0) | (te_ref[i] != te_ref[jnp.maximum(i - 1, 0)]))
    def _():
        wgub[:, :f] = wg_ref[0, 0].astype(BF16)
        wgub[:, f:] = wu_ref[0, 0].astype(BF16)
        wdb[...] = wd_ref[0, 0].astype(BF16)

    @pl.when(i < nu_ref[0])
    def _():
        x = _rms(x_ref[...], g_ref[...]).astype(BF16)
        au = _dot(x, wgub[...])
        a, u = au[:, :f], au[:, f:]
        act = (a * jax.nn.sigmoid(a) * u).astype(BF16)
        o_ref[...] = _dot(act, wdb[...])

    @pl.when(i >= nu_ref[0])
    def _():
        o_ref[...] = jnp.zeros_like(o_ref)


def _moe_experts(xs, g, tile_expert, n_used, w_gate, w_up, w_down, layer, tm):
    n_rows, d = xs.shape
    n_tiles = n_rows // tm
    f = w_gate.shape[3]
    xmap = lambda i, te, nu: (jnp.minimum(i, nu[0] - 1), 0)
    grid_spec = pltpu.PrefetchScalarGridSpec(
        num_scalar_prefetch=2,
        grid=(n_tiles,),
        in_specs=[
            pl.BlockSpec((tm, d), xmap),
            pl.BlockSpec((1, d), lambda i, te, nu: (0, 0)),
            pl.BlockSpec((1, 1, d, f), lambda i, te, nu: (layer, te[i], 0, 0)),
            pl.BlockSpec((1, 1, d, f), lambda i, te, nu: (layer, te[i], 0, 0)),
            pl.BlockSpec((1, 1, f, d), lambda i, te, nu: (layer, te[i], 0, 0)),
        ],
        out_specs=pl.BlockSpec((tm, d), lambda i, te, nu: (i, 0)),
        scratch_shapes=[pltpu.VMEM((d, 2 * f), BF16), pltpu.VMEM((f, d), BF16)],
    )
    return pl.pallas_call(
        _moe_expert_kernel,
        grid_spec=grid_spec,
        out_shape=jax.ShapeDtypeStruct((n_rows, d), F32),
        compiler_params=_cp("arbitrary"),
        name="moe_experts",
    )(tile_expert, n_used, xs, g.reshape(1, d), w_gate, w_up, w_down)


def _combine_ple_kernel(pos_ref, h_ref, route_ref, y_hbm, p_ref, win_ref, g_ref, wgate_ref,
                        gfin_ref, o_ref, ybuf, sem, *, tm, final_norm):
    i = pl.program_id(0)
    n = pl.num_programs(0)
    m = n * tm

    def row_copy(tile, slot, k, r):
        return pltpu.make_async_copy(y_hbm.at[pl.ds(pos_ref[k * m + tile * tm + r], 1), :],
                                     ybuf.at[slot, pl.ds(k * tm + r, 1), :], sem.at[slot])

    def start_tile(tile, slot):
        def body(r, c):
            row_copy(tile, slot, 0, r).start(priority=0)
            row_copy(tile, slot, 1, r).start(priority=1)
            return c
        lax.fori_loop(0, tm, body, 0, unroll=8)

    def wait_tile(tile, slot):
        def body(r, c):
            row_copy(tile, slot, 0, r).wait()
            row_copy(tile, slot, 1, r).wait()
            return c
        lax.fori_loop(0, tm, body, 0, unroll=8)

    @pl.when(i == 0)
    def _():
        start_tile(0, 0)

    @pl.when(i + 1 < n)
    def _():
        start_tile(i + 1, (i + 1) % 2)

    slot = i % 2
    wait_tile(i, slot)
    route = route_ref[...]
    h2 = h_ref[...] + route[:, 2:3] * ybuf[slot, 0:tm, :] + route[:, 3:4] * ybuf[slot, tm:2 * tm, :]
    gate = jax.nn.sigmoid(_dot(_rms(h2, g_ref[...]).astype(BF16), wgate_ref[...]))
    h3 = h2 + _dot(p_ref[0].astype(BF16), win_ref[...]) * gate
    if final_norm:
        h3 = _rms(h3, gfin_ref[...])
    o_ref[...] = h3


def _combine_ple(h, route, pos, y_sorted, p_all, layer, win_bf, g_ple, wgate_bf, g_final, tm, final_norm):
    m, d = h.shape
    pd = p_all.shape[2]
    grid_spec = pltpu.PrefetchScalarGridSpec(
        num_scalar_prefetch=1,
        grid=(m // tm,),
        in_specs=[
            pl.BlockSpec((tm, d), lambda i, pos: (i, 0)),
            pl.BlockSpec((tm, LANES), lambda i, pos: (i, 0)),
            pl.BlockSpec(memory_space=pl.ANY),
            pl.BlockSpec((1, tm, pd), lambda i, pos: (layer, i, 0)),
            pl.BlockSpec((pd, d), lambda i, pos: (0, 0)),
            pl.BlockSpec((1, d), lambda i, pos: (0, 0)),
            pl.BlockSpec((d, d), lambda i, pos: (0, 0)),
            pl.BlockSpec((1, d), lambda i, pos: (0, 0)),
        ],
        out_specs=pl.BlockSpec((tm, d), lambda i, pos: (i, 0)),
        scratch_shapes=[pltpu.VMEM((2, 2 * tm, d), F32), pltpu.SemaphoreType.DMA((2,))],
    )
    return pl.pallas_call(
        functools.partial(_combine_ple_kernel, tm=tm, final_norm=final_norm),
        grid_spec=grid_spec,
        out_shape=jax.ShapeDtypeStruct((m, d), F32),
        compiler_params=_cp("arbitrary"),
        name="moe_combine_ple",
    )(pos, h, route, y_sorted, p_all, win_bf, g_ple.reshape(1, d), wgate_bf, g_final.reshape(1, d))


def _moe_ple_layer(h, p_all, norm_ffn, w_group, b_group, w_expert, b_expert, w_gate, w_up, w_down, layer,
                   win_bf, g_ple, wgate_bf, g_final, final_norm):
    m = h.shape[0]
    n_experts = w_expert.shape[1]
    tm_e = 256 if m >= 4096 else 32
    route, counts = _router(h, norm_ffn, w_group, b_group, w_expert, b_expert)
    pos, tile_expert, n_used, pad_start, pad_len = _route_plan(route, counts, tm_e)
    n_rows = 2 * m + n_experts * tm_e
    xs = _moe_dispatch(h, pos, pad_start, pad_len, n_used, n_rows, tm_e)
    y_sorted = _moe_experts(xs, norm_ffn, tile_expert, n_used, w_gate, w_up, w_down, layer, tm_e)
    tm_c = _row_tile(m, 256)
    return _combine_ple(h, route, pos, y_sorted, p_all, layer, win_bf, g_ple, wgate_bf, g_final, tm_c,
                        final_norm)


def _rope_block(x, c, s):
    return x * c + pltpu.roll(x, 64, 1) * s


def _mla_proj_kernel(x_ref, gmix_ref, wdq_ref, gq_ref, wuq_ref, wdkv_ref, gkv_ref, cos_ref, sin_ref,
                     qn_ref, qr_ref, ckv_ref, kr_ref, *, scale, kv_lora, n_pairs):
    hn = _rms(x_ref[...], gmix_ref[...]).astype(BF16)
    cq = _rms(_dot(hn, wdq_ref[...]), gq_ref[...]).astype(BF16)
    q = _dot(cq, wuq_ref[...]) * scale
    nn = n_pairs * LANES
    c, s = cos_ref[...], sin_ref[...]
    qn_ref[...] = q[:, :nn]
    for p in range(n_pairs):
        blk = q[:, nn + p * LANES:nn + (p + 1) * LANES]
        qr_ref[:, p * LANES:(p + 1) * LANES] = _rope_block(blk, c, s)
    kv = _dot(hn, wdkv_ref[...])
    ckv_ref[...] = _rms(kv[:, :kv_lora], gkv_ref[...])
    kr_ref[...] = _rope_block(kv[:, kv_lora:], c, s)


def _mla_proj_prompt_kernel(x_ref, gmix_ref, wdq_ref, gq_ref, wqt_ref, wdkv_ref, gkv_ref, cos_ref, sin_ref,
                            cost_ref, sint_ref, wuk_ref, wuvt_ref,
                            qt_ref, ckv_ref, kr_ref, kn_ref, krb_ref, vt_ref, *, scale, kv_lora, n_pairs):
    tm = x_ref.shape[0]
    half = LANES // 2
    hn = _rms(x_ref[...], gmix_ref[...]).astype(BF16)
    cq = _rms(_dot(hn, wdq_ref[...]), gq_ref[...]).astype(BF16)
    q3 = (_dot_nt(wqt_ref[...], cq) * scale).reshape(n_pairs, 2 * LANES, tm)
    rope = q3[:, LANES:, :]
    rolled = jnp.concatenate([rope[:, half:, :], rope[:, :half, :]], axis=1)
    qt_ref[:, :LANES, :] = q3[:, :LANES, :].astype(BF16)
    qt_ref[:, LANES:, :] = (rope * cost_ref[...][None] + rolled * sint_ref[...][None]).astype(BF16)
    kv = _dot(hn, wdkv_ref[...])
    ckv = _rms(kv[:, :kv_lora], gkv_ref[...])
    kr = _rope_block(kv[:, kv_lora:], cos_ref[...], sin_ref[...])
    ckv_ref[...] = ckv
    kr_ref[...] = kr
    cb = ckv.astype(BF16)
    kn_ref[...] = _dot(cb, wuk_ref[...]).astype(BF16)
    krb_ref[...] = kr.astype(BF16)
    vt_ref[...] = _dot_nt(wuvt_ref[...], cb).reshape(n_pairs, LANES, tm).astype(BF16)


def _rope_tables(pos, rope_dim):
    half = rope_dim // 2
    inv_freq = jnp.power(jnp.float32(ROPE_THETA), -jnp.arange(half, dtype=F32) / half)
    ang = pos.astype(F32)[:, None] * inv_freq[None, :]
    cos, sin = jnp.cos(ang), jnp.sin(ang)
    z = jnp.zeros((pos.shape[0], 2 * half), F32)
    c = jnp.concatenate([cos, cos, z, cos, cos, z], axis=1)
    s = jnp.concatenate([-sin, -sin, z, sin, sin, z], axis=1)
    return c, s


def _mla_weights(w_dq, w_uq, w_dkv, w_uk, w_uv, n_heads, nope, rope_dim, kv_lora):
    half = rope_dim // 2
    assert 2 * nope == LANES and 4 * half <= LANES // 2
    n_pairs = n_heads // 2
    hd = nope + rope_dim
    idx_n = np.array([h * hd + n for h in range(n_heads) for n in range(nope)])
    w_qn = w_uq[:, idx_n]
    zq = jnp.zeros((w_uq.shape[0], LANES // 2 - 2 * half), F32)
    blocks = []
    for p in range(n_pairs):
        h0, h1 = 2 * p, 2 * p + 1
        x1 = lambda h: w_uq[:, h * hd + nope:h * hd + nope + half]
        x2 = lambda h: w_uq[:, h * hd + nope + half:h * hd + hd]
        blocks += [x1(h0), x1(h1), zq, x2(h0), x2(h1), zq]
    w_qr = jnp.concatenate(blocks, axis=1)
    w_q = jnp.concatenate([w_qn, w_qr], axis=1).astype(BF16)
    w_qt = jnp.concatenate(
        [jnp.concatenate([w_qn[:, p * LANES:(p + 1) * LANES].T, w_qr[:, p * LANES:(p + 1) * LANES].T], axis=0)
         for p in range(n_pairs)], axis=0).astype(BF16)
    zk = jnp.zeros((w_dkv.shape[0], LANES // 2 - 2 * half), F32)
    k1 = w_dkv[:, kv_lora:kv_lora + half]
    k2 = w_dkv[:, kv_lora + half:]
    w_kv = jnp.concatenate([w_dkv[:, :kv_lora], k1, k1, zk, k2, k2, zk], axis=1).astype(BF16)
    w_k = w_uk.reshape(kv_lora, n_heads * nope).astype(BF16)
    w_vt = w_uv.reshape(kv_lora, -1).T.astype(BF16)
    return w_dq.astype(BF16), w_q, w_kv, w_k, w_qt, w_vt


def _mla_proj(h, g_mix, wts, g_q, g_kv, cos, sin, scale):
    m, d = h.shape
    w_dq, w_q, w_kv, w_k = wts[:4]
    q_lora = w_dq.shape[1]
    kv_lora = w_k.shape[0]
    n_pairs = w_k.shape[1] // LANES
    nn = n_pairs * LANES
    tm = _row_tile(m, 512)
    tab = pl.BlockSpec((1, LANES), lambda i: (0, 0))
    const = lambda a: pl.BlockSpec(a.shape, lambda i: (0,) * a.ndim)
    row = lambda n: pl.BlockSpec((tm, n), lambda i: (i, 0))
    gq2, gkv2, gm2 = g_q.reshape(1, q_lora), g_kv.reshape(1, kv_lora), g_mix.reshape(1, d)
    return pl.pallas_call(
        functools.partial(_mla_proj_kernel, scale=scale, kv_lora=kv_lora, n_pairs=n_pairs),
        grid=(m // tm,),
        in_specs=[row(d), const(gm2), const(w_dq), const(gq2), const(w_q), const(w_kv), const(gkv2),
                  tab, tab],
        out_specs=[row(nn), row(nn), row(kv_lora), row(LANES)],
        out_shape=[jax.ShapeDtypeStruct((m, nn), F32), jax.ShapeDtypeStruct((m, nn), F32),
                   jax.ShapeDtypeStruct((m, kv_lora), F32), jax.ShapeDtypeStruct((m, LANES), F32)],
        compiler_params=_cp("parallel"),
        name="mla_proj",
    )(h, gm2, w_dq, gq2, w_q, w_kv, gkv2, cos, sin)


def _mla_proj_prompt(h, g_mix, wts, g_q, g_kv, cos, sin, scale):
    m, d = h.shape
    w_dq, _, w_kv, w_k, w_qt, w_vt = wts
    q_lora = w_dq.shape[1]
    kv_lora = w_k.shape[0]
    n_pairs = w_k.shape[1] // LANES
    nn = n_pairs * LANES
    t = cos.shape[0]
    tm = _row_tile(t, 512)
    nt = t // tm
    tab = pl.BlockSpec((tm, LANES), lambda i: (i % nt, 0))
    tab_t = pl.BlockSpec((LANES, tm), lambda i: (0, i % nt))
    const = lambda a: pl.BlockSpec(a.shape, lambda i: (0,) * a.ndim)
    row = lambda n: pl.BlockSpec((tm, n), lambda i: (i, 0))
    col = lambda n: pl.BlockSpec((n_pairs, n, tm), lambda i: (0, 0, i))
    gq2, gkv2, gm2 = g_q.reshape(1, q_lora), g_kv.reshape(1, kv_lora), g_mix.reshape(1, d)
    return pl.pallas_call(
        functools.partial(_mla_proj_prompt_kernel, scale=scale, kv_lora=kv_lora, n_pairs=n_pairs),
        grid=(m // tm,),
        in_specs=[row(d), const(gm2), const(w_dq), const(gq2), const(w_qt), const(w_kv), const(gkv2),
                  tab, tab, tab_t, tab_t, const(w_k), const(w_vt)],
        out_specs=[col(2 * LANES), row(kv_lora), row(LANES), row(nn), row(LANES), col(LANES)],
        out_shape=[jax.ShapeDtypeStruct((n_pairs, 2 * LANES, m), BF16),
                   jax.ShapeDtypeStruct((m, kv_lora), F32), jax.ShapeDtypeStruct((m, LANES), F32),
                   jax.ShapeDtypeStruct((m, nn), BF16), jax.ShapeDtypeStruct((m, LANES), BF16),
                   jax.ShapeDtypeStruct((n_pairs, LANES, m), BF16)],
        compiler_params=_cp("parallel"),
        name="mla_proj_prompt",
    )(h, gm2, w_dq, gq2, w_qt, w_kv, gkv2, cos, sin, cos.T, sin.T, w_k, w_vt)


def _attn_prompt_kernel(qt_ref, kn_ref, kr_ref, vt_ref, o_ref, *, tq, tk):
    i = pl.program_id(2)
    half = LANES // 2
    nk = tq // tk
    qt = qt_ref[0].astype(F32)
    row = lax.broadcasted_iota(jnp.int32, qt.shape, 0)
    ws = []
    for hh in range(2):
        sel = (((row < LANES) & ((row // half) == hh))
               | ((row >= LANES) & (((row % half) // 16) == hh)))
        ws.append(jnp.where(sel, qt, 0.0).astype(BF16))

    ones = jnp.ones((16, tk), BF16)

    def block(j, carry, diag):
        r0 = pl.multiple_of(j * tk, tk)
        k = jnp.concatenate([kn_ref[0, pl.ds(r0, tk), :], kr_ref[0, pl.ds(r0, tk), :]], axis=1)
        vt = vt_ref[0, :, pl.ds(r0, tk)]
        out = []
        for hh in range(2):
            m, acc = carry[hh]
            st = _dot(k, ws[hh])
            if diag is not None:
                key = lax.broadcasted_iota(jnp.int32, st.shape, 0) + diag * tk
                qry = lax.broadcasted_iota(jnp.int32, st.shape, 1)
                st = jnp.where(key <= qry, st, -jnp.inf)
            m_new = jnp.maximum(m, jnp.max(st, axis=0, keepdims=True))
            alpha = jnp.exp2(m - m_new)
            pt = jnp.exp2(st - m_new).astype(BF16)
            va = jnp.concatenate([vt[hh * half:(hh + 1) * half, :], ones], axis=0)
            out.append((m_new, alpha * acc + _dot(va, pt)))
        return tuple(out)

    init = tuple((jnp.full((1, tq), -jnp.inf, F32), jnp.zeros((half + 16, tq), F32)) for _ in range(2))
    carry = lax.fori_loop(0, i * nk, lambda j, c: block(j, c, None), init)
    for dd in range(nk):
        carry = block(i * nk + dd, carry, dd)
    (_, a0), (_, a1) = carry
    ot = jnp.concatenate([a0[:half] / a0[half:half + 1], a1[:half] / a1[half:half + 1]], axis=0)
    o_ref[0] = ot.T.astype(o_ref.dtype)


def _attn_prompt(qt, kn, kr, vt, bsz, t):
    n_pairs = qt.shape[0]
    tq = _row_tile(t, 1024)
    tk = tq
    nq = t // tq
    r3 = lambda a: a.reshape(bsz, t, a.shape[-1])
    return pl.pallas_call(
        functools.partial(_attn_prompt_kernel, tq=tq, tk=tk),
        grid=(bsz, n_pairs, nq),
        in_specs=[pl.BlockSpec((1, 2 * LANES, tq), lambda b, p, i: (p, 0, b * nq + i)),
                  pl.BlockSpec((1, t, LANES), lambda b, p, i: (b, 0, p)),
                  pl.BlockSpec((1, t, LANES), lambda b, p, i: (b, 0, 0)),
                  pl.BlockSpec((1, LANES, t), lambda b, p, i: (p, 0, b))],
        out_specs=pl.BlockSpec((1, tq, LANES), lambda b, p, i: (b, i, p)),
        out_shape=jax.ShapeDtypeStruct((bsz, t, n_pairs * LANES), BF16),
        compiler_params=_cp("parallel", "parallel", "arbitrary"),
        name="attn_prompt",
    )(qt, r3(kn), r3(kr), vt)


def _out_proj_kernel(h_ref, o_ref, w_ref, y_ref):
    y_ref[...] = h_ref[...] + _dot(o_ref[...], w_ref[...])


def _out_proj(h, o_bf, wo_bf):
    m, d = h.shape
    n = o_bf.shape[1]
    tm = _row_tile(m, 512)
    return pl.pallas_call(
        _out_proj_kernel,
        grid=(m // tm,),
        in_specs=[pl.BlockSpec((tm, d), lambda i: (i, 0)), pl.BlockSpec((tm, n), lambda i: (i, 0)),
                  pl.BlockSpec((n, d), lambda i: (0, 0))],
        out_specs=pl.BlockSpec((tm, d), lambda i: (i, 0)),
        out_shape=jax.ShapeDtypeStruct((m, d), F32),
        compiler_params=_cp("parallel"),
        name="attn_out_proj",
    )(h, o_bf, wo_bf)


def _q_absorb_kernel(qn_ref, wukt_ref, o_ref, *, n_pairs, kv_lora):
    lane = lax.broadcasted_iota(jnp.int32, (1, LANES), 1)
    half = LANES // 2
    for p in range(n_pairs):
        blk = qn_ref[:, p * LANES:(p + 1) * LANES]
        for hh in range(2):
            qm = jnp.where((lane // half) == hh, blk, 0.0).astype(BF16)
            h = 2 * p + hh
            o_ref[:, h * kv_lora:(h + 1) * kv_lora] = _dot(qm, wukt_ref[p])


def _q_absorb(qn, wukt_bf):
    m = qn.shape[0]
    n_pairs, _, kv_lora = wukt_bf.shape
    return pl.pallas_call(
        functools.partial(_q_absorb_kernel, n_pairs=n_pairs, kv_lora=kv_lora),
        out_shape=jax.ShapeDtypeStruct((m, 2 * n_pairs * kv_lora), F32),
        compiler_params=pltpu.CompilerParams(vmem_limit_bytes=VMEM_LIMIT),
        name="q_absorb",
    )(qn, wukt_bf)


def _attn_decode_kernel(pt_ref, qlat_ref, qpe_ref, cnew_ref, knew_ref, ckv_hbm, kpe_hbm, o_ref,
                        cbuf, kbuf, sem, *, n_pages, page, layer, n_chunks):
    b = pl.program_id(0)
    nb = pl.num_programs(0)

    def copies(seq, slot, j):
        pg = pt_ref[seq, j]
        return (pltpu.make_async_copy(ckv_hbm.at[pg, layer], cbuf.at[slot, pl.ds(j * page, page), :],
                                      sem.at[0, slot]),
                pltpu.make_async_copy(kpe_hbm.at[pg, layer], kbuf.at[slot, :, pl.ds(j * page, page)],
                                      sem.at[1, slot]))

    def start_seq(seq, slot):
        def body(j, c):
            for cp in copies(seq, slot, j):
                cp.start()
            return c
        lax.fori_loop(0, n_pages, body, 0, unroll=8)

    def wait_seq(seq, slot):
        def body(j, c):
            for cp in copies(seq, slot, j):
                cp.wait()
            return c
        lax.fori_loop(0, n_pages, body, 0, unroll=8)

    @pl.when(b == 0)
    def _():
        start_seq(0, 0)

    @pl.when(b + 1 < nb)
    def _():
        start_seq(b + 1, (b + 1) % 2)

    slot = b % 2
    wait_seq(b, slot)
    ql = qlat_ref[0].astype(BF16)
    qp = qpe_ref[0].astype(BF16)
    cnew = cnew_ref[0].astype(BF16).astype(F32)
    knew = knew_ref[0].astype(BF16).astype(F32)
    s_new = (jnp.sum(ql.astype(F32) * cnew, axis=-1, keepdims=True)
             + jnp.sum(qp.astype(F32) * knew, axis=-1, keepdims=True))
    ch = (n_pages * page) // n_chunks

    def chunk(state, c):
        m, l, acc = state
        ck = cbuf[slot, c * ch:(c + 1) * ch, :].astype(BF16)
        kt = kbuf[slot, :, c * ch:(c + 1) * ch].astype(BF16)
        s = _dot_nt(ql, ck) + _dot(qp, kt)
        m_new = jnp.maximum(m, jnp.max(s, axis=-1, keepdims=True))
        alpha = jnp.exp(m - m_new)
        p = jnp.exp(s - m_new)
        return (m_new, alpha * l + jnp.sum(p, axis=-1, keepdims=True),
                alpha * acc + _dot(p.astype(BF16), ck))

    n_a = (n_chunks + 1) // 2
    st_a = (s_new, jnp.ones_like(s_new), jnp.broadcast_to(cnew, (ql.shape[0], cnew.shape[1])))
    for c in range(n_a):
        st_a = chunk(st_a, c)
    if n_chunks > n_a:
        st_b = (jnp.full_like(s_new, -jnp.inf), jnp.zeros_like(s_new), jnp.zeros_like(st_a[2]))
        for c in range(n_a, n_chunks):
            st_b = chunk(st_b, c)
        m = jnp.maximum(st_a[0], st_b[0])
        fa, fb = jnp.exp(st_a[0] - m), jnp.exp(st_b[0] - m)
        l = fa * st_a[1] + fb * st_b[1]
        acc = fa * st_a[2] + fb * st_b[2]
    else:
        _, l, acc = st_a
    o_ref[0] = acc / l


def _attn_decode(page_table, qlat, qpe, cnew, knew, cache_ckv, cache_kpe_t, layer):
    bsz, n_heads, kv_lora = qlat.shape
    rope_dim = qpe.shape[-1]
    n_pages = page_table.shape[1]
    page = cache_ckv.shape[2]
    n_chunks = 4 if (n_pages * page) % (4 * LANES) == 0 else 1
    per_seq = lambda n, w: pl.BlockSpec((1, n, w), lambda b, pt: (b, 0, 0))
    grid_spec = pltpu.PrefetchScalarGridSpec(
        num_scalar_prefetch=1,
        grid=(bsz,),
        in_specs=[per_seq(n_heads, kv_lora), per_seq(n_heads, rope_dim), per_seq(1, kv_lora),
                  per_seq(1, rope_dim), pl.BlockSpec(memory_space=pl.ANY),
                  pl.BlockSpec(memory_space=pl.ANY)],
        out_specs=per_seq(n_heads, kv_lora),
        scratch_shapes=[pltpu.VMEM((2, n_pages * page, kv_lora), F32),
                        pltpu.VMEM((2, rope_dim, n_pages * page), F32),
                        pltpu.SemaphoreType.DMA((2, 2))],
    )
    return pl.pallas_call(
        functools.partial(_attn_decode_kernel, n_pages=n_pages, page=page, layer=layer,
                          n_chunks=n_chunks),
        grid_spec=grid_spec,
        out_shape=jax.ShapeDtypeStruct((bsz, n_heads, kv_lora), F32),
        compiler_params=_cp("arbitrary"),
        name="attn_decode",
    )(page_table, qlat, qpe, cnew, knew, cache_ckv, cache_kpe_t)


def _decode_out_kernel(h_ref, olat_ref, wuv_ref, wo_ref, y_ref, o_scr, *, n_pairs, kv_lora):
    for p in range(n_pairs):
        blk = olat_ref[:, 2 * p * kv_lora:(2 * p + 2) * kv_lora].astype(BF16)
        o_scr[:, p * LANES:(p + 1) * LANES] = _dot(blk, wuv_ref[p]).astype(BF16)
    y_ref[...] = h_ref[...] + _dot(o_scr[...], wo_ref[...])


def _decode_out(h, olat, wuv_bd_bf, wo_bf):
    m, d = h.shape
    n_pairs = wuv_bd_bf.shape[0]
    kv_lora = wuv_bd_bf.shape[1] // 2
    return pl.pallas_call(
        functools.partial(_decode_out_kernel, n_pairs=n_pairs, kv_lora=kv_lora),
        out_shape=jax.ShapeDtypeStruct((m, d), F32),
        scratch_shapes=[pltpu.VMEM((m, n_pairs * LANES), BF16)],
        compiler_params=pltpu.CompilerParams(vmem_limit_bytes=VMEM_LIMIT),
        name="decode_out_proj",
    )(h, olat, wuv_bd_bf, wo_bf)


def kernel(x_prompt, x_sample, p_prompt, p_sample, state_conv, cache_ckv, cache_kpe, page_table, norm_mix, norm_ffn, norm_final, conv_w_pw1, conv_b_pw1, conv_w_dw, conv_b_dw, conv_ln_g, conv_ln_b, conv_w_pw2, mla_w_dq, mla_g_q, mla_w_uq, mla_w_dkv, mla_g_kv, mla_w_uk, mla_w_uv, mla_w_o, moe_w_group, moe_b_group, moe_w_expert, moe_b_expert, moe_w_gate, moe_w_up, moe_w_down, ple_w_in, ple_g, ple_w_gate):
    bsz, t, d = x_prompt.shape
    dbsz, dt, _ = x_sample.shape
    depth = p_prompt.shape[0]
    assert depth == 2 and dt == 1
    kv_lora, n_heads, nope = mla_w_uk.shape[1:]
    rope_dim = mla_w_dkv.shape[2] - kv_lora
    vdim = mla_w_uv.shape[3]
    past_len = page_table.shape[1] * cache_ckv.shape[2]
    scale = 1.0 / math.sqrt(nope + rope_dim)
    width = conv_w_dw.shape[1]
    bf = lambda a: a.astype(BF16)

    mp, ms = bsz * t, dbsz * dt
    hp = x_prompt.reshape(mp, d)
    hs = x_sample.reshape(ms, d)
    pp_all = p_prompt.reshape(depth, mp, -1)
    ps_all = p_sample.reshape(depth, ms, -1)

    w1, w2 = bf(conv_w_pw1[0]), bf(conv_w_pw2[0])
    conv_args = (conv_w_dw[0], conv_b_dw[0], conv_ln_g[0], conv_ln_b[0], w2)
    glu_p = _pw1_glu(hp, norm_mix[0], w1, conv_b_pw1[0]).reshape(bsz, t, d)
    hp = _conv_prompt(glu_p, hp.reshape(bsz, t, d), *conv_args).reshape(mp, d)
    conv_state_prompt = glu_p[:, t - (width - 1):][None]
    glu_s = _pw1_glu(hs, norm_mix[0], w1, conv_b_pw1[0])
    buf = state_conv[0]
    hs = _conv_sample(jnp.swapaxes(buf, 0, 1), glu_s, hs, *conv_args)
    conv_state_sample = jnp.concatenate([buf[:, 1:], glu_s[:, None, :]], axis=1)[None]

    def moe_ple(h, p_all, i, final_norm):
        return _moe_ple_layer(h, p_all, norm_ffn[i], moe_w_group[i], moe_b_group[i], moe_w_expert[i],
                              moe_b_expert[i], moe_w_gate, moe_w_up, moe_w_down, i,
                              bf(ple_w_in[i]), ple_g[i], bf(ple_w_gate[i]), norm_final, final_norm)

    hp = moe_ple(hp, pp_all, 0, False)
    hs = moe_ple(hs, ps_all, 0, False)

    wts = _mla_weights(mla_w_dq[0], mla_w_uq[0], mla_w_dkv[0], mla_w_uk[0], mla_w_uv[0],
                       n_heads, nope, rope_dim, kv_lora)
    wo = bf(mla_w_o[0])
    half = rope_dim // 2
    natural = lambda kr: jnp.concatenate([kr[:, :half], kr[:, LANES // 2:LANES // 2 + half]], axis=1)

    cos_p, sin_p = _rope_tables(jnp.arange(t, dtype=jnp.int32), rope_dim)
    qt, ckv_p, kr_p, kn, krb, vt = _mla_proj_prompt(hp, norm_mix[1], wts, mla_g_q[0], mla_g_kv[0],
                                                    cos_p, sin_p, scale * math.log2(math.e))
    o_p = _attn_prompt(qt, kn, krb, vt, bsz, t)
    hp = _out_proj(hp, o_p.reshape(mp, -1), wo)
    ckv_prompt = ckv_p.reshape(1, bsz, t, kv_lora)
    kpe_prompt = natural(kr_p).reshape(1, bsz, t, rope_dim)

    cos_s, sin_s = _rope_tables(past_len + jnp.arange(dt, dtype=jnp.int32), rope_dim)
    qn_s, qr_s, ckv_s, kr_s = _mla_proj(hs, norm_mix[1], wts, mla_g_q[0], mla_g_kv[0],
                                        cos_s, sin_s, scale)
    n_pairs = n_heads // 2
    wukt = bf(jnp.transpose(mla_w_uk[0], (1, 2, 0)).reshape(n_pairs, 2 * nope, kv_lora))
    qlat = _q_absorb(qn_s, wukt).reshape(ms, n_heads, kv_lora)
    qr3 = qr_s.reshape(ms, n_pairs, LANES)
    x1 = qr3[:, :, :2 * half].reshape(ms, n_heads, half)
    x2 = qr3[:, :, LANES // 2:LANES // 2 + 2 * half].reshape(ms, n_heads, half)
    qpe = jnp.concatenate([x1, x2], axis=-1)
    kpe_s = natural(kr_s)
    olat = _attn_decode(page_table, qlat, qpe, ckv_s.reshape(ms, 1, kv_lora),
                        kpe_s.reshape(ms, 1, rope_dim), cache_ckv, jnp.swapaxes(cache_kpe, 2, 3), 0)
    wuv = mla_w_uv[0]
    zv = jnp.zeros((kv_lora, vdim), F32)
    wuv_bd = jnp.stack([jnp.concatenate([jnp.concatenate([wuv[:, 2 * p], zv], axis=1),
                                         jnp.concatenate([zv, wuv[:, 2 * p + 1]], axis=1)], axis=0)
                        for p in range(n_pairs)])
    hs = _decode_out(hs, olat.reshape(ms, n_heads * kv_lora), bf(wuv_bd), wo)
    ckv_sample = ckv_s.reshape(1, dbsz, dt, kv_lora)
    kpe_sample = kpe_s.reshape(1, dbsz, dt, rope_dim)

    y_prompt = moe_ple(hp, pp_all, 1, True).reshape(bsz, t, d)
    y_sample = moe_ple(hs, ps_all, 1, True).reshape(dbsz, dt, d)
    return (y_prompt, y_sample, conv_state_prompt, conv_state_sample,
            ckv_prompt, kpe_prompt, ckv_sample, kpe_sample)
```

```python
import functools
import math

import jax
import jax.numpy as jnp
import numpy as np
from jax import lax
from jax.experimental import pallas as pl
from jax.experimental.pallas import tpu as pltpu

F32 = jnp.float32
BF16 = jnp.bfloat16

EPS = 1e-6
ROPE_THETA = 10000.0
CONV_HALO = 32
LANES = 128
VMEM_LIMIT = 56 * 1024 * 1024


def _cp(*sem):
    return pltpu.CompilerParams(dimension_semantics=sem, vmem_limit_bytes=VMEM_LIMIT)


def _rms(x, g):
    return x * lax.rsqrt(jnp.mean(x * x, axis=-1, keepdims=True) + EPS) * g


def _dot(a, b):
    return jnp.dot(a, b, preferred_element_type=F32)


def _dot_nt(a, b):
    return lax.dot_general(a, b, (((1,), (1,)), ((), ())), preferred_element_type=F32)


def _row_tile(m, target):
    t = min(m, target)
    while m % t:
        t //= 2
    return t


def _pw1_glu_kernel(x_ref, g_ref, w_ref, b_ref, o_ref):
    d = o_ref.shape[-1]
    hn = _rms(x_ref[...], g_ref[...]).astype(BF16)
    a = _dot(hn, w_ref[:, :d]) + b_ref[:, :d]
    b = _dot(hn, w_ref[:, d:]) + b_ref[:, d:]
    o_ref[...] = a * jax.nn.sigmoid(b)


def _pw1_glu(x, g, w_bf, b):
    m, d = x.shape
    tm = _row_tile(m, 512)
    return pl.pallas_call(
        _pw1_glu_kernel,
        grid=(m // tm,),
        in_specs=[
            pl.BlockSpec((tm, d), lambda i: (i, 0)),
            pl.BlockSpec((1, d), lambda i: (0, 0)),
            pl.BlockSpec((d, 2 * d), lambda i: (0, 0)),
            pl.BlockSpec((1, 2 * d), lambda i: (0, 0)),
        ],
        out_specs=pl.BlockSpec((tm, d), lambda i: (i, 0)),
        out_shape=jax.ShapeDtypeStruct((m, d), F32),
        compiler_params=_cp("parallel"),
        name="pw1_glu",
    )(x, g.reshape(1, d), w_bf, b.reshape(1, 2 * d))


def _ln_silu(z, g, b):
    mu = jnp.mean(z, axis=-1, keepdims=True)
    zc = z - mu
    var = jnp.mean(zc * zc, axis=-1, keepdims=True)
    y = zc * lax.rsqrt(var + EPS) * g + b
    return y * jax.nn.sigmoid(y)


def _conv_prompt_kernel(cur_ref, halo_ref, h_ref, wdw_ref, bdw_ref, lng_ref, lnb_ref, w2_ref,
                        o_ref, full_ref, sh_ref, z_ref, *, width, rows, lanes):
    tq, d = cur_ref.shape[1], cur_ref.shape[2]
    i = pl.program_id(1)
    full_ref[0:CONV_HALO, :] = jnp.where(i > 0, halo_ref[0], 0.0)
    full_ref[CONV_HALO:CONV_HALO + tq, :] = cur_ref[0]
    off = CONV_HALO - (width - 1)
    sub = 8
    for l0 in range(0, d, lanes):
        taps = [len(range(r, width, sub)) for r in range(sub)]
        for r in range(sub):
            n = tq + sub * (taps[r] - 1)
            sh_ref[r, 0:n, :] = full_ref[off + r:off + r + n, l0:l0 + lanes]
        for r0 in range(0, tq, rows):
            acc = jnp.zeros((rows, lanes), F32)
            for k in range(width):
                a, r = divmod(k, sub)
                acc = acc + sh_ref[r, r0 + sub * a:r0 + sub * a + rows, :] * wdw_ref[k:k + 1, l0:l0 + lanes]
            z_ref[r0:r0 + rows, l0:l0 + lanes] = acc
    y = _ln_silu(z_ref[...] + bdw_ref[...], lng_ref[...], lnb_ref[...])
    o_ref[0] = h_ref[0] + _dot(y.astype(BF16), w2_ref[...])


def _conv_prompt(glu, h, w_dw, b_dw, ln_g, ln_b, w2_bf):
    bsz, t, d = glu.shape
    width = w_dw.shape[0]
    tq = _row_tile(t, 512)
    hb = tq // CONV_HALO
    wpad = jnp.zeros((CONV_HALO, d), F32).at[:width].set(w_dw)
    lanes = 256
    kern = functools.partial(_conv_prompt_kernel, width=width, rows=min(64, tq), lanes=lanes)
    vec = lambda a: a.reshape(1, d)
    cvec = pl.BlockSpec((1, d), lambda b, i: (0, 0))
    return pl.pallas_call(
        kern,
        grid=(bsz, t // tq),
        in_specs=[
            pl.BlockSpec((1, tq, d), lambda b, i: (b, i, 0)),
            pl.BlockSpec((1, CONV_HALO, d), lambda b, i: (b, jnp.maximum(i * hb - 1, 0), 0)),
            pl.BlockSpec((1, tq, d), lambda b, i: (b, i, 0)),
            pl.BlockSpec((CONV_HALO, d), lambda b, i: (0, 0)),
            cvec, cvec, cvec,
            pl.BlockSpec((d, d), lambda b, i: (0, 0)),
        ],
        out_specs=pl.BlockSpec((1, tq, d), lambda b, i: (b, i, 0)),
        out_shape=jax.ShapeDtypeStruct((bsz, t, d), F32),
        scratch_shapes=[pltpu.VMEM((tq + CONV_HALO, d), F32),
                        pltpu.VMEM((8, tq + CONV_HALO, lanes), F32), pltpu.VMEM((tq, d), F32)],
        compiler_params=_cp("parallel", "arbitrary"),
        name="conv_prompt",
    )(glu, glu, h, wpad, vec(b_dw), vec(ln_g), vec(ln_b), w2_bf)


def _conv_sample_kernel(buf_ref, u_ref, h_ref, wdw_ref, bdw_ref, lng_ref, lnb_ref, w2_ref, o_ref,
                        *, width):
    acc = u_ref[...] * wdw_ref[width - 1:width, :]
    for k in range(width - 1):
        acc = acc + buf_ref[k] * wdw_ref[k:k + 1, :]
    y = _ln_silu(acc + bdw_ref[...], lng_ref[...], lnb_ref[...])
    o_ref[...] = h_ref[...] + _dot(y.astype(BF16), w2_ref[...])


def _conv_sample(buf_t, u, h, w_dw, b_dw, ln_g, ln_b, w2_bf):
    nb, bsz, d = buf_t.shape
    width = w_dw.shape[0]
    bb = _row_tile(bsz, 32)
    wpad = jnp.zeros((CONV_HALO, d), F32).at[:width].set(w_dw)
    vec = lambda a: a.reshape(1, d)
    cvec = pl.BlockSpec((1, d), lambda i: (0, 0))
    row = pl.BlockSpec((bb, d), lambda i: (i, 0))
    return pl.pallas_call(
        functools.partial(_conv_sample_kernel, width=width),
        grid=(bsz // bb,),
        in_specs=[
            pl.BlockSpec((nb, bb, d), lambda i: (0, i, 0)),
            row, row,
            pl.BlockSpec((CONV_HALO, d), lambda i: (0, 0)),
            cvec, cvec, cvec,
            pl.BlockSpec((d, d), lambda i: (0, 0)),
        ],
        out_specs=row,
        out_shape=jax.ShapeDtypeStruct((bsz, d), F32),
        compiler_params=_cp("parallel"),
        name="conv_sample",
    )(buf_t, u, h, wpad, vec(b_dw), vec(ln_g), vec(ln_b), w2_bf)


def _router_kernel(x_ref, g_ref, whi_ref, wlo_ref, b_ref, o_ref, cnt_ref, *, n_groups, per_group):
    hn = _rms(x_ref[...], g_ref[...])
    hn_hi = hn.astype(BF16)
    hn_lo = (hn - hn_hi.astype(F32)).astype(BF16)
    logits = (_dot(hn_hi, whi_ref[...]) + _dot(hn_lo, whi_ref[...]) + _dot(hn_hi, wlo_ref[...])
              + b_ref[...])
    lane = lax.broadcasted_iota(jnp.int32, logits.shape, 1)
    neg = jnp.float32(-jnp.inf)
    big = jnp.int32(1 << 20)

    def first_argmax(v):
        m = jnp.max(v, axis=-1, keepdims=True)
        return m, jnp.min(jnp.where(v == m, lane, big), axis=-1, keepdims=True)

    gl = jnp.where(lane < n_groups, logits, neg)
    gmax, grp = first_argmax(gl)
    g_w = 1.0 / jnp.sum(jnp.exp(gl - gmax), axis=-1, keepdims=True)
    lo = n_groups + grp * per_group
    el = jnp.where((lane >= lo) & (lane < lo + per_group), logits, neg)
    m1, i1 = first_argmax(el)
    el2 = jnp.where(lane == i1, neg, el)
    m2, i2 = first_argmax(el2)
    e2 = jnp.exp(m2 - m1)
    w1 = g_w / (1.0 + e2)
    w2 = g_w * e2 / (1.0 + e2)
    id1 = (i1 - n_groups).astype(F32)
    id2 = (i2 - n_groups).astype(F32)

    @pl.when(pl.program_id(0) == 0)
    def _():
        cnt_ref[...] = jnp.zeros_like(cnt_ref)

    tm = logits.shape[0]
    picks = ((lane == i1) | (lane == i2)).astype(BF16)
    r_io = lax.broadcasted_iota(jnp.int32, (tm, tm), 0)
    c_io = lax.broadcasted_iota(jnp.int32, (tm, tm), 1)
    before = _dot((c_io < r_io).astype(BF16), picks) + cnt_ref[...]
    rank1 = jnp.sum(jnp.where(lane == i1, before, 0.0), axis=-1, keepdims=True)
    rank2 = jnp.sum(jnp.where(lane == i2, before, 0.0), axis=-1, keepdims=True)
    cnt_ref[...] = cnt_ref[...] + jnp.sum(picks.astype(F32), axis=0, keepdims=True)
    vals = (id1, id2, w1, w2, rank1, rank2)
    out = jnp.zeros_like(logits)
    for k, v in enumerate(vals):
        out = jnp.where(lane == k, v, out)
    o_ref[...] = out


def _router(h, g, w_group, b_group, w_expert, b_expert):
    m, d = h.shape
    n_groups, n_experts = w_group.shape[1], w_expert.shape[1]
    tm = _row_tile(m, 512)
    w = jnp.zeros((d, LANES), F32).at[:, :n_groups].set(w_group)
    w = w.at[:, n_groups:n_groups + n_experts].set(w_expert)
    b = jnp.zeros((1, LANES), F32).at[0, :n_groups].set(b_group)
    b = b.at[0, n_groups:n_groups + n_experts].set(b_expert)
    kern = functools.partial(_router_kernel, n_groups=n_groups, per_group=n_experts // n_groups)
    w_hi = w.astype(BF16)
    w_lo = (w - w_hi.astype(F32)).astype(BF16)
    route, counts = pl.pallas_call(
        kern,
        grid=(m // tm,),
        in_specs=[
            pl.BlockSpec((tm, d), lambda i: (i, 0)),
            pl.BlockSpec((1, d), lambda i: (0, 0)),
            pl.BlockSpec((d, LANES), lambda i: (0, 0)),
            pl.BlockSpec((d, LANES), lambda i: (0, 0)),
            pl.BlockSpec((1, LANES), lambda i: (0, 0)),
        ],
        out_specs=[pl.BlockSpec((tm, LANES), lambda i: (i, 0)), pl.BlockSpec((1, LANES), lambda i: (0, 0))],
        out_shape=[jax.ShapeDtypeStruct((m, LANES), F32), jax.ShapeDtypeStruct((1, LANES), F32)],
        compiler_params=_cp("arbitrary"),
        name="moe_router",
    )(h, g.reshape(1, d), w_hi, w_lo, b)
    return route, counts[0, n_groups:n_groups + n_experts].astype(jnp.int32)


def _route_plan(route, counts, tm):
    m = route.shape[0]
    n_experts = counts.shape[0]
    experts = jnp.arange(n_experts, dtype=jnp.int32)
    tiles_per = (counts + tm - 1) // tm
    tile_end = jnp.cumsum(tiles_per)
    start = (tile_end - tiles_per) * tm
    eid = route[:, :2].astype(jnp.int32).T
    rank = route[:, 4:6].astype(jnp.int32).T
    pos = rank + jnp.sum(jnp.where(eid[..., None] == experts, start, 0), axis=-1)
    n_tiles = (2 * m) // tm + n_experts
    tile_ids = jnp.arange(n_tiles, dtype=jnp.int32)
    tile_expert = jnp.minimum(jnp.sum((tile_end[None, :] <= tile_ids[:, None]).astype(jnp.int32), axis=1),
                              n_experts - 1)
    n_used = tile_end[-1:].astype(jnp.int32)
    pad_start = (start + counts).astype(jnp.int32)
    pad_len = (tiles_per * tm - counts).astype(jnp.int32)
    return pos.reshape(-1).astype(jnp.int32), tile_expert.astype(jnp.int32), n_used, pad_start, pad_len


def _moe_dispatch_kernel(pos_ref, ps_ref, pn_ref, nu_ref, h_ref, xs_hbm, zbuf, sem, *, td, m, n_experts, tm,
                         n_tiles):
    i = pl.program_id(0)

    def row_copy(k, r):
        return pltpu.make_async_copy(h_ref.at[pl.ds(r, 1), :],
                                     xs_hbm.at[pl.ds(pos_ref[k * m + i * td + r], 1), :], sem.at[0])

    def pad_copy(e, j):
        return pltpu.make_async_copy(zbuf.at[pl.ds(0, 1), :], xs_hbm.at[pl.ds(ps_ref[e] + j, 1), :], sem.at[1])

    def tail_copy(t):
        return pltpu.make_async_copy(zbuf, xs_hbm.at[pl.ds(pl.multiple_of(t * tm, tm), tm), :], sem.at[1])

    def for_tail(fn):
        lax.fori_loop(nu_ref[0], n_tiles, lambda t, c: (fn(tail_copy(t)), c)[1], 0)

    def for_rows(fn):
        def body(r, c):
            fn(row_copy(0, r), 0)
            fn(row_copy(1, r), 1)
            return c
        lax.fori_loop(0, td, body, 0, unroll=8)

    def for_pads(fn):
        def per_expert(e, c):
            lax.fori_loop(0, pn_ref[e], lambda j, c2: (fn(pad_copy(e, j)), c2)[1], 0)
            return c
        lax.fori_loop(0, n_experts, per_expert, 0)

    for_rows(lambda cp, k: cp.start(priority=k))

    @pl.when(i == 0)
    def _():
        zbuf[...] = jnp.zeros_like(zbuf)
        for_pads(lambda cp: cp.start())
        for_tail(lambda cp: cp.start())
        for_pads(lambda cp: cp.wait())
        for_tail(lambda cp: cp.wait())

    for_rows(lambda cp, k: cp.wait())


def _moe_dispatch(h, pos, pad_start, pad_len, n_used, n_rows, tm):
    m, d = h.shape
    td = _row_tile(m, 512)
    n_experts = pad_start.shape[0]
    grid_spec = pltpu.PrefetchScalarGridSpec(
        num_scalar_prefetch=4,
        grid=(m // td,),
        in_specs=[pl.BlockSpec((td, d), lambda i, pos, ps, pn, nu: (i, 0))],
        out_specs=pl.BlockSpec(memory_space=pl.ANY),
        scratch_shapes=[pltpu.VMEM((tm, d), F32), pltpu.SemaphoreType.DMA((2,))],
    )
    return pl.pallas_call(
        functools.partial(_moe_dispatch_kernel, td=td, m=m, n_experts=n_experts, tm=tm,
                          n_tiles=n_rows // tm),
        grid_spec=grid_spec,
        out_shape=jax.ShapeDtypeStruct((n_rows, d), F32),
        compiler_params=_cp("arbitrary"),
        name="moe_dispatch",
    )(pos, pad_start, pad_len, n_used, h)


def _moe_expert_kernel(te_ref, nu_ref, x_ref, g_ref, wg_ref, wu_ref, wd_ref, o_ref, wgub, wdb):
    i = pl.program_id(0)
    f = wdb.shape[0]

    @pl.when((i == 0) | (te_ref[i] != te_ref[jnp.maximum(i - 1, 0)]))
    def _():
        wgub[:, :f] = wg_ref[0, 0].astype(BF16)
        wgub[:, f:] = wu_ref[0, 0].astype(BF16)
        wdb[...] = wd_ref[0, 0].astype(BF16)

    @pl.when(i < nu_ref[0])
    def _():
        x = _rms(x_ref[...], g_ref[...]).astype(BF16)
        au = _dot(x, wgub[...])
        a, u = au[:, :f], au[:, f:]
        act = (a * jax.nn.sigmoid(a) * u).astype(BF16)
        o_ref[...] = _dot(act, wdb[...])

    @pl.when(i >= nu_ref[0])
    def _():
        o_ref[...] = jnp.zeros_like(o_ref)


def _moe_experts(xs, g, tile_expert, n_used, w_gate, w_up, w_down, layer, tm):
    n_rows, d = xs.shape
    n_tiles = n_rows // tm
    f = w_gate.shape[3]
    xmap = lambda i, te, nu: (jnp.minimum(i, nu[0] - 1), 0)
    grid_spec = pltpu.PrefetchScalarGridSpec(
        num_scalar_prefetch=2,
        grid=(n_tiles,),
        in_specs=[
            pl.BlockSpec((tm, d), xmap),
            pl.BlockSpec((1, d), lambda i, te, nu: (0, 0)),
            pl.BlockSpec((1, 1, d, f), lambda i, te, nu: (layer, te[i], 0, 0)),
            pl.BlockSpec((1, 1, d, f), lambda i, te, nu: (layer, te[i], 0, 0)),
            pl.BlockSpec((1, 1, f, d), lambda i, te, nu: (layer, te[i], 0, 0)),
        ],
        out_specs=pl.BlockSpec((tm, d), lambda i, te, nu: (i, 0)),
        scratch_shapes=[pltpu.VMEM((d, 2 * f), BF16), pltpu.VMEM((f, d), BF16)],
    )
    return pl.pallas_call(
        _moe_expert_kernel,
        grid_spec=grid_spec,
        out_shape=jax.ShapeDtypeStruct((n_rows, d), F32),
        compiler_params=_cp("arbitrary"),
        name="moe_experts",
    )(tile_expert, n_used, xs, g.reshape(1, d), w_gate, w_up, w_down)


def _combine_ple_kernel(pos_ref, h_ref, route_ref, y_hbm, p_ref, win_ref, g_ref, wgate_ref,
                        gfin_ref, o_ref, ybuf, sem, *, tm, final_norm):
    i = pl.program_id(0)
    n = pl.num_programs(0)
    m = n * tm

    def row_copy(tile, slot, k, r):
        return pltpu.make_async_copy(y_hbm.at[pl.ds(pos_ref[k * m + tile * tm + r], 1), :],
                                     ybuf.at[slot, pl.ds(k * tm + r, 1), :], sem.at[slot])

    def start_tile(tile, slot):
        def body(r, c):
            row_copy(tile, slot, 0, r).start(priority=0)
            row_copy(tile, slot, 1, r).start(priority=1)
            return c
        lax.fori_loop(0, tm, body, 0, unroll=8)

    def wait_tile(tile, slot):
        def body(r, c):
            row_copy(tile, slot, 0, r).wait()
            row_copy(tile, slot, 1, r).wait()
            return c
        lax.fori_loop(0, tm, body, 0, unroll=8)

    @pl.when(i == 0)
    def _():
        start_tile(0, 0)

    @pl.when(i + 1 < n)
    def _():
        start_tile(i + 1, (i + 1) % 2)

    slot = i % 2
    wait_tile(i, slot)
    route = route_ref[...]
    h2 = h_ref[...] + route[:, 2:3] * ybuf[slot, 0:tm, :] + route[:, 3:4] * ybuf[slot, tm:2 * tm, :]
    gate = jax.nn.sigmoid(_dot(_rms(h2, g_ref[...]).astype(BF16), wgate_ref[...]))
    h3 = h2 + _dot(p_ref[0].astype(BF16), win_ref[...]) * gate
    if final_norm:
        h3 = _rms(h3, gfin_ref[...])
    o_ref[...] = h3


def _combine_ple(h, route, pos, y_sorted, p_all, layer, win_bf, g_ple, wgate_bf, g_final, tm, final_norm):
    m, d = h.shape
    pd = p_all.shape[2]
    grid_spec = pltpu.PrefetchScalarGridSpec(
        num_scalar_prefetch=1,
        grid=(m // tm,),
        in_specs=[
            pl.BlockSpec((tm, d), lambda i, pos: (i, 0)),
            pl.BlockSpec((tm, LANES), lambda i, pos: (i, 0)),
            pl.BlockSpec(memory_space=pl.ANY),
            pl.BlockSpec((1, tm, pd), lambda i, pos: (layer, i, 0)),
            pl.BlockSpec((pd, d), lambda i, pos: (0, 0)),
            pl.BlockSpec((1, d), lambda i, pos: (0, 0)),
            pl.BlockSpec((d, d), lambda i, pos: (0, 0)),
            pl.BlockSpec((1, d), lambda i, pos: (0, 0)),
        ],
        out_specs=pl.BlockSpec((tm, d), lambda i, pos: (i, 0)),
        scratch_shapes=[pltpu.VMEM((2, 2 * tm, d), F32), pltpu.SemaphoreType.DMA((2,))],
    )
    return pl.pallas_call(
        functools.partial(_combine_ple_kernel, tm=tm, final_norm=final_norm),
        grid_spec=grid_spec,
        out_shape=jax.ShapeDtypeStruct((m, d), F32),
        compiler_params=_cp("arbitrary"),
        name="moe_combine_ple",
    )(pos, h, route, y_sorted, p_all, win_bf, g_ple.reshape(1, d), wgate_bf, g_final.reshape(1, d))


def _moe_ple_layer(h, p_all, norm_ffn, w_group, b_group, w_expert, b_expert, w_gate, w_up, w_down, layer,
                   win_bf, g_ple, wgate_bf, g_final, final_norm):
    m = h.shape[0]
    n_experts = w_expert.shape[1]
    tm_e = 256 if m >= 4096 else 32
    route, counts = _router(h, norm_ffn, w_group, b_group, w_expert, b_expert)
    pos, tile_expert, n_used, pad_start, pad_len = _route_plan(route, counts, tm_e)
    n_rows = 2 * m + n_experts * tm_e
    xs = _moe_dispatch(h, pos, pad_start, pad_len, n_used, n_rows, tm_e)
    y_sorted = _moe_experts(xs, norm_ffn, tile_expert, n_used, w_gate, w_up, w_down, layer, tm_e)
    tm_c = _row_tile(m, 256)
    return _combine_ple(h, route, pos, y_sorted, p_all, layer, win_bf, g_ple, wgate_bf, g_final, tm_c,
                        final_norm)


def _rope_block(x, c, s):
    return x * c + pltpu.roll(x, 64, 1) * s


def _mla_proj_kernel(x_ref, gmix_ref, wdq_ref, gq_ref, wuq_ref, wdkv_ref, gkv_ref, cos_ref, sin_ref,
                     qn_ref, qr_ref, ckv_ref, kr_ref, *, scale, kv_lora, n_pairs):
    hn = _rms(x_ref[...], gmix_ref[...]).astype(BF16)
    cq = _rms(_dot(hn, wdq_ref[...]), gq_ref[...]).astype(BF16)
    q = _dot(cq, wuq_ref[...]) * scale
    nn = n_pairs * LANES
    c, s = cos_ref[...], sin_ref[...]
    qn_ref[...] = q[:, :nn]
    for p in range(n_pairs):
        blk = q[:, nn + p * LANES:nn + (p + 1) * LANES]
        qr_ref[:, p * LANES:(p + 1) * LANES] = _rope_block(blk, c, s)
    kv = _dot(hn, wdkv_ref[...])
    ckv_ref[...] = _rms(kv[:, :kv_lora], gkv_ref[...])
    kr_ref[...] = _rope_block(kv[:, kv_lora:], c, s)


def _mla_proj_prompt_kernel(x_ref, gmix_ref, wdq_ref, gq_ref, wqt_ref, wdkv_ref, gkv_ref, cos_ref, sin_ref,
                            cost_ref, sint_ref, wuk_ref, wuvt_ref,
                            qt_ref, ckv_ref, kr_ref, kn_ref, krb_ref, vt_ref, *, scale, kv_lora, n_pairs):
    tm = x_ref.shape[0]
    half = LANES // 2
    hn = _rms(x_ref[...], gmix_ref[...]).astype(BF16)
    cq = _rms(_dot(hn, wdq_ref[...]), gq_ref[...]).astype(BF16)
    q3 = (_dot_nt(wqt_ref[...], cq) * scale).reshape(n_pairs, 2 * LANES, tm)
    rope = q3[:, LANES:, :]
    rolled = jnp.concatenate([rope[:, half:, :], rope[:, :half, :]], axis=1)
    qt_ref[:, :LANES, :] = q3[:, :LANES, :].astype(BF16)
    qt_ref[:, LANES:, :] = (rope * cost_ref[...][None] + rolled * sint_ref[...][None]).astype(BF16)
    kv = _dot(hn, wdkv_ref[...])
    ckv = _rms(kv[:, :kv_lora], gkv_ref[...])
    kr = _rope_block(kv[:, kv_lora:], cos_ref[...], sin_ref[...])
    ckv_ref[...] = ckv
    kr_ref[...] = kr
    cb = ckv.astype(BF16)
    kn_ref[...] = _dot(cb, wuk_ref[...]).astype(BF16)
    krb_ref[...] = kr.astype(BF16)
    vt_ref[...] = _dot_nt(wuvt_ref[...], cb).reshape(n_pairs, LANES, tm).astype(BF16)


def _rope_tables(pos, rope_dim):
    half = rope_dim // 2
    inv_freq = jnp.power(jnp.float32(ROPE_THETA), -jnp.arange(half, dtype=F32) / half)
    ang = pos.astype(F32)[:, None] * inv_freq[None, :]
    cos, sin = jnp.cos(ang), jnp.sin(ang)
    z = jnp.zeros((pos.shape[0], 2 * half), F32)
    c = jnp.concatenate([cos, cos, z, cos, cos, z], axis=1)
    s = jnp.concatenate([-sin, -sin, z, sin, sin, z], axis=1)
    return c, s


def _mla_weights(w_dq, w_uq, w_dkv, w_uk, w_uv, n_heads, nope, rope_dim, kv_lora):
    half = rope_dim // 2
    assert 2 * nope == LANES and 4 * half <= LANES // 2
    n_pairs = n_heads // 2
    hd = nope + rope_dim
    idx_n = np.array([h * hd + n for h in range(n_heads) for n in range(nope)])
    w_qn = w_uq[:, idx_n]
    zq = jnp.zeros((w_uq.shape[0], LANES // 2 - 2 * half), F32)
    blocks = []
    for p in range(n_pairs):
        h0, h1 = 2 * p, 2 * p + 1
        x1 = lambda h: w_uq[:, h * hd + nope:h * hd + nope + half]
        x2 = lambda h: w_uq[:, h * hd + nope + half:h * hd + hd]
        blocks += [x1(h0), x1(h1), zq, x2(h0), x2(h1), zq]
    w_qr = jnp.concatenate(blocks, axis=1)
    w_q = jnp.concatenate([w_qn, w_qr], axis=1).astype(BF16)
    w_qt = jnp.concatenate(
        [jnp.concatenate([w_qn[:, p * LANES:(p + 1) * LANES].T, w_qr[:, p * LANES:(p + 1) * LANES].T], axis=0)
         for p in range(n_pairs)], axis=0).astype(BF16)
    zk = jnp.zeros((w_dkv.shape[0], LANES // 2 - 2 * half), F32)
    k1 = w_dkv[:, kv_lora:kv_lora + half]
    k2 = w_dkv[:, kv_lora + half:]
    w_kv = jnp.concatenate([w_dkv[:, :kv_lora], k1, k1, zk, k2, k2, zk], axis=1).astype(BF16)
    w_k = w_uk.reshape(kv_lora, n_heads * nope).astype(BF16)
    w_vt = w_uv.reshape(kv_lora, -1).T.astype(BF16)
    return w_dq.astype(BF16), w_q, w_kv, w_k, w_qt, w_vt


def _mla_proj(h, g_mix, wts, g_q, g_kv, cos, sin, scale):
    m, d = h.shape
    w_dq, w_q, w_kv, w_k = wts[:4]
    q_lora = w_dq.shape[1]
    kv_lora = w_k.shape[0]
    n_pairs = w_k.shape[1] // LANES
    nn = n_pairs * LANES
    tm = _row_tile(m, 512)
    tab = pl.BlockSpec((1, LANES), lambda i: (0, 0))
    const = lambda a: pl.BlockSpec(a.shape, lambda i: (0,) * a.ndim)
    row = lambda n: pl.BlockSpec((tm, n), lambda i: (i, 0))
    gq2, gkv2, gm2 = g_q.reshape(1, q_lora), g_kv.reshape(1, kv_lora), g_mix.reshape(1, d)
    return pl.pallas_call(
        functools.partial(_mla_proj_kernel, scale=scale, kv_lora=kv_lora, n_pairs=n_pairs),
        grid=(m // tm,),
        in_specs=[row(d), const(gm2), const(w_dq), const(gq2), const(w_q), const(w_kv), const(gkv2),
                  tab, tab],
        out_specs=[row(nn), row(nn), row(kv_lora), row(LANES)],
        out_shape=[jax.ShapeDtypeStruct((m, nn), F32), jax.ShapeDtypeStruct((m, nn), F32),
                   jax.ShapeDtypeStruct((m, kv_lora), F32), jax.ShapeDtypeStruct((m, LANES), F32)],
        compiler_params=_cp("parallel"),
        name="mla_proj",
    )(h, gm2, w_dq, gq2, w_q, w_kv, gkv2, cos, sin)


def _mla_proj_prompt(h, g_mix, wts, g_q, g_kv, cos, sin, scale):
    m, d = h.shape
    w_dq, _, w_kv, w_k, w_qt, w_vt = wts
    q_lora = w_dq.shape[1]
    kv_lora = w_k.shape[0]
    n_pairs = w_k.shape[1] // LANES
    nn = n_pairs * LANES
    t = cos.shape[0]
    tm = _row_tile(t, 512)
    nt = t // tm
    tab = pl.BlockSpec((tm, LANES), lambda i: (i % nt, 0))
    tab_t = pl.BlockSpec((LANES, tm), lambda i: (0, i % nt))
    const = lambda a: pl.BlockSpec(a.shape, lambda i: (0,) * a.ndim)
    row = lambda n: pl.BlockSpec((tm, n), lambda i: (i, 0))
    col = lambda n: pl.BlockSpec((n_pairs, n, tm), lambda i: (0, 0, i))
    gq2, gkv2, gm2 = g_q.reshape(1, q_lora), g_kv.reshape(1, kv_lora), g_mix.reshape(1, d)
    return pl.pallas_call(
        functools.partial(_mla_proj_prompt_kernel, scale=scale, kv_lora=kv_lora, n_pairs=n_pairs),
        grid=(m // tm,),
        in_specs=[row(d), const(gm2), const(w_dq), const(gq2), const(w_qt), const(w_kv), const(gkv2),
                  tab, tab, tab_t, tab_t, const(w_k), const(w_vt)],
        out_specs=[col(2 * LANES), row(kv_lora), row(LANES), row(nn), row(LANES), col(LANES)],
        out_shape=[jax.ShapeDtypeStruct((n_pairs, 2 * LANES, m), BF16),
                   jax.ShapeDtypeStruct((m, kv_lora), F32), jax.ShapeDtypeStruct((m, LANES), F32),
                   jax.ShapeDtypeStruct((m, nn), BF16), jax.ShapeDtypeStruct((m, LANES), BF16),
                   jax.ShapeDtypeStruct((n_pairs, LANES, m), BF16)],
        compiler_params=_cp("parallel"),
        name="mla_proj_prompt",
    )(h, gm2, w_dq, gq2, w_qt, w_kv, gkv2, cos, sin, cos.T, sin.T, w_k, w_vt)


def _attn_prompt_kernel(qt_ref, kn_ref, kr_ref, vt_ref, o_ref, *, tq, tk, npp):
    i = pl.program_id(2)
    half = LANES // 2
    nk = tq // tk
    ws = []
    for pp in range(npp):
        qt = qt_ref[pp].astype(F32)
        row = lax.broadcasted_iota(jnp.int32, qt.shape, 0)
        for hh in range(2):
            sel = (((row < LANES) & ((row // half) == hh))
                   | ((row >= LANES) & (((row % half) // 16) == hh)))
            ws.append(jnp.where(sel, qt, 0.0).astype(BF16))

    ones = jnp.ones((16, tk), BF16)

    def block(j, carry, diag):
        r0 = pl.multiple_of(j * tk, tk)
        kr = kr_ref[0, pl.ds(r0, tk), :]
        out = []
        for pp in range(npp):
            k = jnp.concatenate([kn_ref[0, pl.ds(r0, tk), pp * LANES:(pp + 1) * LANES], kr], axis=1)
            vt = vt_ref[pp, :, pl.ds(r0, tk)]
            for hh in range(2):
                m, acc = carry[2 * pp + hh]
                st = _dot(k, ws[2 * pp + hh])
                if diag is not None:
                    key = lax.broadcasted_iota(jnp.int32, st.shape, 0) + diag * tk
                    qry = lax.broadcasted_iota(jnp.int32, st.shape, 1)
                    st = jnp.where(key <= qry, st, -jnp.inf)
                m_new = jnp.maximum(m, jnp.max(st, axis=0, keepdims=True))
                alpha = jnp.exp2(m - m_new)
                pt = jnp.exp2(st - m_new).astype(BF16)
                va = jnp.concatenate([vt[hh * half:(hh + 1) * half, :], ones], axis=0)
                out.append((m_new, alpha * acc + _dot(va, pt)))
        return tuple(out)

    init = tuple((jnp.full((1, tq), -jnp.inf, F32), jnp.zeros((half + 16, tq), F32)) for _ in range(2 * npp))
    carry = lax.fori_loop(0, i * nk, lambda j, c: block(j, c, None), init)
    for dd in range(nk):
        carry = block(i * nk + dd, carry, dd)
    for pp in range(npp):
        (_, a0), (_, a1) = carry[2 * pp], carry[2 * pp + 1]
        ot = jnp.concatenate([a0[:half] / a0[half:half + 1], a1[:half] / a1[half:half + 1]], axis=0)
        o_ref[0, :, pp * LANES:(pp + 1) * LANES] = ot.T.astype(o_ref.dtype)


def _attn_prompt(qt, kn, kr, vt, bsz, t):
    n_pairs = qt.shape[0]
    npp = 2 if n_pairs % 2 == 0 else 1
    tq = _row_tile(t, 1024)
    tk = tq
    nq = t // tq
    r3 = lambda a: a.reshape(bsz, t, a.shape[-1])
    return pl.pallas_call(
        functools.partial(_attn_prompt_kernel, tq=tq, tk=tk, npp=npp),
        grid=(bsz, n_pairs // npp, nq),
        in_specs=[pl.BlockSpec((npp, 2 * LANES, tq), lambda b, p, i: (p, 0, b * nq + i)),
                  pl.BlockSpec((1, t, npp * LANES), lambda b, p, i: (b, 0, p)),
                  pl.BlockSpec((1, t, LANES), lambda b, p, i: (b, 0, 0)),
                  pl.BlockSpec((npp, LANES, t), lambda b, p, i: (p, 0, b))],
        out_specs=pl.BlockSpec((1, tq, npp * LANES), lambda b, p, i: (b, i, p)),
        out_shape=jax.ShapeDtypeStruct((bsz, t, n_pairs * LANES), BF16),
        compiler_params=_cp("parallel", "parallel", "arbitrary"),
        name="attn_prompt",
    )(qt, r3(kn), r3(kr), vt)


def _out_proj_kernel(h_ref, o_ref, w_ref, y_ref):
    y_ref[...] = h_ref[...] + _dot(o_ref[...], w_ref[...])


def _out_proj(h, o_bf, wo_bf):
    m, d = h.shape
    n = o_bf.shape[1]
    tm = _row_tile(m, 512)
    return pl.pallas_call(
        _out_proj_kernel,
        grid=(m // tm,),
        in_specs=[pl.BlockSpec((tm, d), lambda i: (i, 0)), pl.BlockSpec((tm, n), lambda i: (i, 0)),
                  pl.BlockSpec((n, d), lambda i: (0, 0))],
        out_specs=pl.BlockSpec((tm, d), lambda i: (i, 0)),
        out_shape=jax.ShapeDtypeStruct((m, d), F32),
        compiler_params=_cp("parallel"),
        name="attn_out_proj",
    )(h, o_bf, wo_bf)


def _q_absorb_kernel(qn_ref, wukt_ref, o_ref, *, n_pairs, kv_lora):
    lane = lax.broadcasted_iota(jnp.int32, (1, LANES), 1)
    half = LANES // 2
    for p in range(n_pairs):
        blk = qn_ref[:, p * LANES:(p + 1) * LANES]
        for hh in range(2):
            qm = jnp.where((lane // half) == hh, blk, 0.0).astype(BF16)
            h = 2 * p + hh
            o_ref[:, h * kv_lora:(h + 1) * kv_lora] = _dot(qm, wukt_ref[p])


def _q_absorb(qn, wukt_bf):
    m = qn.shape[0]
    n_pairs, _, kv_lora = wukt_bf.shape
    return pl.pallas_call(
        functools.partial(_q_absorb_kernel, n_pairs=n_pairs, kv_lora=kv_lora),
        out_shape=jax.ShapeDtypeStruct((m, 2 * n_pairs * kv_lora), F32),
        compiler_params=pltpu.CompilerParams(vmem_limit_bytes=VMEM_LIMIT),
        name="q_absorb",
    )(qn, wukt_bf)


def _attn_decode_kernel(pt_ref, qlat_ref, qpe_ref, cnew_ref, knew_ref, ckv_hbm, kpe_hbm, o_ref,
                        cbuf, kbuf, sem, *, n_pages, page, layer, n_chunks):
    b = pl.program_id(0)
    nb = pl.num_programs(0)

    def copies(seq, slot, j):
        pg = pt_ref[seq, j]
        return (pltpu.make_async_copy(ckv_hbm.at[pg, layer], cbuf.at[slot, pl.ds(j * page, page), :],
                                      sem.at[0, slot]),
                pltpu.make_async_copy(kpe_hbm.at[pg, layer], kbuf.at[slot, :, pl.ds(j * page, page)],
                                      sem.at[1, slot]))

    def start_seq(seq, slot):
        def body(j, c):
            for cp in copies(seq, slot, j):
                cp.start()
            return c
        lax.fori_loop(0, n_pages, body, 0, unroll=8)

    def wait_seq(seq, slot):
        def body(j, c):
            for cp in copies(seq, slot, j):
                cp.wait()
            return c
        lax.fori_loop(0, n_pages, body, 0, unroll=8)

    @pl.when(b == 0)
    def _():
        start_seq(0, 0)

    @pl.when(b + 1 < nb)
    def _():
        start_seq(b + 1, (b + 1) % 2)

    slot = b % 2
    wait_seq(b, slot)
    ql = qlat_ref[0].astype(BF16)
    qp = qpe_ref[0].astype(BF16)
    cnew = cnew_ref[0].astype(BF16).astype(F32)
    knew = knew_ref[0].astype(BF16).astype(F32)
    s_new = (jnp.sum(ql.astype(F32) * cnew, axis=-1, keepdims=True)
             + jnp.sum(qp.astype(F32) * knew, axis=-1, keepdims=True))
    ch = (n_pages * page) // n_chunks

    def chunk(state, c):
        m, l, acc = state
        ck = cbuf[slot, c * ch:(c + 1) * ch, :].astype(BF16)
        kt = kbuf[slot, :, c * ch:(c + 1) * ch].astype(BF16)
        s = _dot_nt(ql, ck) + _dot(qp, kt)
        m_new = jnp.maximum(m, jnp.max(s, axis=-1, keepdims=True))
        alpha = jnp.exp(m - m_new)
        p = jnp.exp(s - m_new)
        return (m_new, alpha * l + jnp.sum(p, axis=-1, keepdims=True),
                alpha * acc + _dot(p.astype(BF16), ck))

    n_a = (n_chunks + 1) // 2
    st_a = (s_new, jnp.ones_like(s_new), jnp.broadcast_to(cnew, (ql.shape[0], cnew.shape[1])))
    for c in range(n_a):
        st_a = chunk(st_a, c)
    if n_chunks > n_a:
        st_b = (jnp.full_like(s_new, -jnp.inf), jnp.zeros_like(s_new), jnp.zeros_like(st_a[2]))
        for c in range(n_a, n_chunks):
            st_b = chunk(st_b, c)
        m = jnp.maximum(st_a[0], st_b[0])
        fa, fb = jnp.exp(st_a[0] - m), jnp.exp(st_b[0] - m)
        l = fa * st_a[1] + fb * st_b[1]
        acc = fa * st_a[2] + fb * st_b[2]
    else:
        _, l, acc = st_a
    o_ref[0] = acc / l


def _attn_decode(page_table, qlat, qpe, cnew, knew, cache_ckv, cache_kpe_t, layer):
    bsz, n_heads, kv_lora = qlat.shape
    rope_dim = qpe.shape[-1]
    n_pages = page_table.shape[1]
    page = cache_ckv.shape[2]
    n_chunks = 4 if (n_pages * page) % (4 * LANES) == 0 else 1
    per_seq = lambda n, w: pl.BlockSpec((1, n, w), lambda b, pt: (b, 0, 0))
    grid_spec = pltpu.PrefetchScalarGridSpec(
        num_scalar_prefetch=1,
        grid=(bsz,),
        in_specs=[per_seq(n_heads, kv_lora), per_seq(n_heads, rope_dim), per_seq(1, kv_lora),
                  per_seq(1, rope_dim), pl.BlockSpec(memory_space=pl.ANY),
                  pl.BlockSpec(memory_space=pl.ANY)],
        out_specs=per_seq(n_heads, kv_lora),
        scratch_shapes=[pltpu.VMEM((2, n_pages * page, kv_lora), F32),
                        pltpu.VMEM((2, rope_dim, n_pages * page), F32),
                        pltpu.SemaphoreType.DMA((2, 2))],
    )
    return pl.pallas_call(
        functools.partial(_attn_decode_kernel, n_pages=n_pages, page=page, layer=layer,
                          n_chunks=n_chunks),
        grid_spec=grid_spec,
        out_shape=jax.ShapeDtypeStruct((bsz, n_heads, kv_lora), F32),
        compiler_params=_cp("arbitrary"),
        name="attn_decode",
    )(page_table, qlat, qpe, cnew, knew, cache_ckv, cache_kpe_t)


def _decode_out_kernel(h_ref, olat_ref, wuv_ref, wo_ref, y_ref, o_scr, *, n_pairs, kv_lora):
    for p in range(n_pairs):
        blk = olat_ref[:, 2 * p * kv_lora:(2 * p + 2) * kv_lora].astype(BF16)
        o_scr[:, p * LANES:(p + 1) * LANES] = _dot(blk, wuv_ref[p]).astype(BF16)
    y_ref[...] = h_ref[...] + _dot(o_scr[...], wo_ref[...])


def _decode_out(h, olat, wuv_bd_bf, wo_bf):
    m, d = h.shape
    n_pairs = wuv_bd_bf.shape[0]
    kv_lora = wuv_bd_bf.shape[1] // 2
    return pl.pallas_call(
        functools.partial(_decode_out_kernel, n_pairs=n_pairs, kv_lora=kv_lora),
        out_shape=jax.ShapeDtypeStruct((m, d), F32),
        scratch_shapes=[pltpu.VMEM((m, n_pairs * LANES), BF16)],
        compiler_params=pltpu.CompilerParams(vmem_limit_bytes=VMEM_LIMIT),
        name="decode_out_proj",
    )(h, olat, wuv_bd_bf, wo_bf)


def kernel(x_prompt, x_sample, p_prompt, p_sample, state_conv, cache_ckv, cache_kpe, page_table, norm_mix, norm_ffn, norm_final, conv_w_pw1, conv_b_pw1, conv_w_dw, conv_b_dw, conv_ln_g, conv_ln_b, conv_w_pw2, mla_w_dq, mla_g_q, mla_w_uq, mla_w_dkv, mla_g_kv, mla_w_uk, mla_w_uv, mla_w_o, moe_w_group, moe_b_group, moe_w_expert, moe_b_expert, moe_w_gate, moe_w_up, moe_w_down, ple_w_in, ple_g, ple_w_gate):
    bsz, t, d = x_prompt.shape
    dbsz, dt, _ = x_sample.shape
    depth = p_prompt.shape[0]
    assert depth == 2 and dt == 1
    kv_lora, n_heads, nope = mla_w_uk.shape[1:]
    rope_dim = mla_w_dkv.shape[2] - kv_lora
    vdim = mla_w_uv.shape[3]
    past_len = page_table.shape[1] * cache_ckv.shape[2]
    scale = 1.0 / math.sqrt(nope + rope_dim)
    width = conv_w_dw.shape[1]
    bf = lambda a: a.astype(BF16)

    mp, ms = bsz * t, dbsz * dt
    hp = x_prompt.reshape(mp, d)
    hs = x_sample.reshape(ms, d)
    pp_all = p_prompt.reshape(depth, mp, -1)
    ps_all = p_sample.reshape(depth, ms, -1)

    w1, w2 = bf(conv_w_pw1[0]), bf(conv_w_pw2[0])
    conv_args = (conv_w_dw[0], conv_b_dw[0], conv_ln_g[0], conv_ln_b[0], w2)
    glu_p = _pw1_glu(hp, norm_mix[0], w1, conv_b_pw1[0]).reshape(bsz, t, d)
    hp = _conv_prompt(glu_p, hp.reshape(bsz, t, d), *conv_args).reshape(mp, d)
    conv_state_prompt = glu_p[:, t - (width - 1):][None]
    glu_s = _pw1_glu(hs, norm_mix[0], w1, conv_b_pw1[0])
    buf = state_conv[0]
    hs = _conv_sample(jnp.swapaxes(buf, 0, 1), glu_s, hs, *conv_args)
    conv_state_sample = jnp.concatenate([buf[:, 1:], glu_s[:, None, :]], axis=1)[None]

    def moe_ple(h, p_all, i, final_norm):
        return _moe_ple_layer(h, p_all, norm_ffn[i], moe_w_group[i], moe_b_group[i], moe_w_expert[i],
                              moe_b_expert[i], moe_w_gate, moe_w_up, moe_w_down, i,
                              bf(ple_w_in[i]), ple_g[i], bf(ple_w_gate[i]), norm_final, final_norm)

    hp = moe_ple(hp, pp_all, 0, False)
    hs = moe_ple(hs, ps_all, 0, False)

    wts = _mla_weights(mla_w_dq[0], mla_w_uq[0], mla_w_dkv[0], mla_w_uk[0], mla_w_uv[0],
                       n_heads, nope, rope_dim, kv_lora)
    wo = bf(mla_w_o[0])
    half = rope_dim // 2
    natural = lambda kr: jnp.concatenate([kr[:, :half], kr[:, LANES // 2:LANES // 2 + half]], axis=1)

    cos_p, sin_p = _rope_tables(jnp.arange(t, dtype=jnp.int32), rope_dim)
    qt, ckv_p, kr_p, kn, krb, vt = _mla_proj_prompt(hp, norm_mix[1], wts, mla_g_q[0], mla_g_kv[0],
                                                    cos_p, sin_p, scale * math.log2(math.e))
    o_p = _attn_prompt(qt, kn, krb, vt, bsz, t)
    hp = _out_proj(hp, o_p.reshape(mp, -1), wo)
    ckv_prompt = ckv_p.reshape(1, bsz, t, kv_lora)
    kpe_prompt = natural(kr_p).reshape(1, bsz, t, rope_dim)

    cos_s, sin_s = _rope_tables(past_len + jnp.arange(dt, dtype=jnp.int32), rope_dim)
    qn_s, qr_s, ckv_s, kr_s = _mla_proj(hs, norm_mix[1], wts, mla_g_q[0], mla_g_kv[0],
                                        cos_s, sin_s, scale)
    n_pairs = n_heads // 2
    wukt = bf(jnp.transpose(mla_w_uk[0], (1, 2, 0)).reshape(n_pairs, 2 * nope, kv_lora))
    qlat = _q_absorb(qn_s, wukt).reshape(ms, n_heads, kv_lora)
    qr3 = qr_s.reshape(ms, n_pairs, LANES)
    x1 = qr3[:, :, :2 * half].reshape(ms, n_heads, half)
    x2 = qr3[:, :, LANES // 2:LANES // 2 + 2 * half].reshape(ms, n_heads, half)
    qpe = jnp.concatenate([x1, x2], axis=-1)
    kpe_s = natural(kr_s)
    olat = _attn_decode(page_table, qlat, qpe, ckv_s.reshape(ms, 1, kv_lora),
                        kpe_s.reshape(ms, 1, rope_dim), cache_ckv, jnp.swapaxes(cache_kpe, 2, 3), 0)
    wuv = mla_w_uv[0]
    zv = jnp.zeros((kv_lora, vdim), F32)
    wuv_bd = jnp.stack([jnp.concatenate([jnp.concatenate([wuv[:, 2 * p], zv], axis=1),
                                         jnp.concatenate([zv, wuv[:, 2 * p + 1]], axis=1)], axis=0)
                        for p in range(n_pairs)])
    hs = _decode_out(hs, olat.reshape(ms, n_heads * kv_lora), bf(wuv_bd), wo)
    ckv_sample = ckv_s.reshape(1, dbsz, dt, kv_lora)
    kpe_sample = kpe_s.reshape(1, dbsz, dt, rope_dim)

    y_prompt = moe_ple(hp, pp_all, 1, True).reshape(bsz, t, d)
    y_sample = moe_ple(hs, ps_all, 1, True).reshape(dbsz, dt, d)
    return (y_prompt, y_sample, conv_state_prompt, conv_state_sample,
            ckv_prompt, kpe_prompt, ckv_sample, kpe_sample)
```
